```python
import math
import jax, jax.numpy as jnp
from jax import lax
import numpy as np

D_MODEL = 1024
BATCH = 8
SEQ = 4096
DEPTH = 1

GRID_W = 64
CTX_LEN = 256
D_HYENA = 512
D_GLA = D_MODEL - D_HYENA
GLA_HEADS = 4
GLA_DK = D_GLA // 2
GLA_DV = D_GLA
GLA_HEAD_DK = GLA_DK // GLA_HEADS
GLA_HEAD_DV = GLA_DV // GLA_HEADS
GATE_RANK = 16
GATE_NORMALIZER = 16.0
GLA_CHUNK = 64
SHORT_CONV = 3
FILTER_EMB = 33
FILTER_BANDS = (FILTER_EMB - 1) // 2
FILTER_HIDDEN = 64
FAST_DECAY_PCT = 0.3
SLOW_DECAY_PCT = 1.5
DECAY_TARGET = 1e-2
D_FF = 4 * D_MODEL
EPS = 1e-6
PROJ_WIDTHS = (3 * D_HYENA, GLA_DK, GLA_DK, GLA_DV, GLA_DV, GATE_RANK, GATE_RANK)
PROJ_SPLITS = tuple(int(s) for s in np.cumsum(PROJ_WIDTHS)[:-1])
N_IN = sum(PROJ_WIDTHS)

kernel_name = 'hybrid_hyena_gla_prefix_dit'


def _rms_norm(x, g):
    xf = x.astype(jnp.float32)
    y = xf * lax.rsqrt(jnp.mean(xf * xf, axis=-1, keepdims=True) + EPS)
    return (y * g.astype(jnp.float32)).astype(x.dtype)


def _sq_relu_mlp(h, w1, w2):
    return jnp.square(jax.nn.relu(h @ w1)) @ w2


def _short_conv(u, w, b):
    t = u.shape[-2]
    pad = SHORT_CONV // 2
    up = jnp.pad(u, [(0, 0)] * (u.ndim - 2) + [(pad, pad), (0, 0)])
    out = b
    for j in range(SHORT_CONV):
        out = out + w[j] * up[..., j:j + t, :]
    return out


def _hyena_spectrum(length, w1, b1, w2, b2, w3, b3, freq, wout):
    f32 = jnp.float32
    t = jnp.linspace(0.0, 1.0, length, dtype=f32)[:, None]
    w = 2.0 * math.pi * jnp.arange(length, dtype=f32) / length
    f = jnp.linspace(1e-4, FILTER_BANDS - 1, FILTER_BANDS, dtype=f32)
    ang = w[:, None] * f[None, :]
    z = jnp.concatenate([t, jnp.cos(ang), -jnp.sin(ang)], axis=-1)
    fr = freq.astype(f32)
    hdn = jnp.sin(fr * (z @ w1.astype(f32) + b1.astype(f32)))
    hdn = jnp.sin(fr * (hdn @ w2.astype(f32) + b2.astype(f32)))
    hdn = jnp.sin(fr * (hdn @ w3.astype(f32) + b3.astype(f32)))
    h = (hdn @ wout.astype(f32)).reshape(length, 2, D_HYENA)
    min_decay = math.log(DECAY_TARGET) / SLOW_DECAY_PCT
    max_decay = math.log(DECAY_TARGET) / FAST_DECAY_PCT
    deltas = jnp.abs(jnp.linspace(min_decay, max_decay, D_HYENA, dtype=f32))
    h = h * jnp.exp(-t * deltas)[:, None, :]
    kern = jnp.concatenate([h[:, 0], jnp.zeros((1, D_HYENA), f32), h[:0:-1, 1]], axis=0)
    kern = kern * lax.rsqrt(jnp.sum(kern * kern, axis=0, keepdims=True) + EPS)
    return jnp.fft.rfft(kern, axis=0)


def _long_conv(u, spec, bias):
    length = u.shape[1]
    uf = u.astype(jnp.float32)
    y = jnp.fft.irfft(jnp.fft.rfft(uf, n=2 * length, axis=1) * spec, n=2 * length, axis=1)[:, :length]
    return (y + uf * bias.astype(jnp.float32)).astype(u.dtype)


def _hyena_gate_conv(u, spec, bias):
    x0, x1, v = jnp.split(u, 3, axis=-1)
    return x0 * _long_conv(x1 * v, spec, bias)


def _to_heads(z, n_heads):
    b, t, w = z.shape
    return z.reshape(b, t, n_heads, w // n_heads).transpose(0, 2, 1, 3)


def _log_gate(gr, w, b):
    lg = jax.nn.log_sigmoid(gr.astype(jnp.float32) @ w.astype(jnp.float32) + b.astype(jnp.float32))
    return _to_heads(lg / GATE_NORMALIZER, GLA_HEADS)


def _gla_chunked(q, k, v, lg, s0):
    bsz, nh, t, dk = q.shape
    dv = v.shape[-1]
    n = t // GLA_CHUNK
    q = q.reshape(bsz, nh, n, GLA_CHUNK, dk)
    k = k.reshape(bsz, nh, n, GLA_CHUNK, dk)
    lg = lg.reshape(bsz, nh, n, GLA_CHUNK, dk)
    v = v.reshape(bsz, nh, n, GLA_CHUNK, dv)
    b = jnp.cumsum(lg, axis=3)
    b_last = b[:, :, :, -1:, :]
    q_in = q * jnp.exp(b)
    k_in = k * jnp.exp(-b)
    k_end = k * jnp.exp(b_last - b)
    mask = jnp.tril(jnp.ones((GLA_CHUNK, GLA_CHUNK), dtype=bool))
    att = jnp.where(mask, jnp.einsum('bhncd,bhnsd->bhncs', q_in, k_in), 0.0)
    o_intra = jnp.einsum('bhncs,bhnse->bhnce', att, v)
    ds = jnp.einsum('bhnsd,bhnse->nbhde', k_end, v)
    decay = jnp.moveaxis(jnp.exp(b_last[:, :, :, 0, :]), 2, 0)

    def step(s, inp):
        d, dsn = inp
        return d[..., None] * s + dsn, s

    s_final, s_start = lax.scan(step, s0, (decay, ds))
    o_inter = jnp.einsum('bhncd,nbhde->bhnce', q_in, s_start)
    return (o_intra + o_inter).reshape(bsz, nh, t, dv), s_final


def _gla_final_state(k, v, lg):
    b = jnp.cumsum(lg, axis=2)
    return jnp.einsum('bhtd,bhte->bhde', k * jnp.exp(b[:, :, -1:] - b), v)


def _rev(a):
    return jnp.flip(a, axis=2)


def _gla_bidir(q, k, v, lg_f, lg_b, s_f0, s_b0):
    o_f, s_f = _gla_chunked(q, k, v, lg_f, s_f0)
    o_b, s_b = _gla_chunked(_rev(q), _rev(k), _rev(v), _rev(lg_b), s_b0)
    return o_f + _rev(o_b), s_f, s_b


def _gla_heads(q, k, v):
    f32 = jnp.float32
    qh = _to_heads(q.astype(f32), GLA_HEADS) * (GLA_HEAD_DK ** -0.5)
    return qh, _to_heads(k.astype(f32), GLA_HEADS), _to_heads(v.astype(f32), GLA_HEADS)


def _gla_output(o, og, g_norm):
    on = o * lax.rsqrt(jnp.mean(o * o, axis=-1, keepdims=True) + EPS) * g_norm.astype(jnp.float32)
    bsz, _, t, _ = o.shape
    y = on.transpose(0, 2, 1, 3).reshape(bsz, t, GLA_DV)
    return (y * jax.nn.silu(og.astype(jnp.float32))).astype(og.dtype)


def setup_inputs(seed: int = 0) -> dict:
    key = jax.random.key(seed)
    ks = jax.random.split(key, 29)
    D = D_MODEL
    L = DEPTH

    def nrm(i, shape, scale):
        return scale * jax.random.normal(ks[i], shape, jnp.float32)

    return {
        'x': nrm(0, (BATCH, SEQ, D), 1.0),
        'c': nrm(1, (BATCH, D), 1.0),
        'ctx': nrm(2, (BATCH, CTX_LEN, D), 1.0),
        'c_ctx': nrm(3, (D,), 1.0),
        'w_ada': nrm(4, (L, D, 6 * D), 0.5 * D ** -0.5),
        'b_ada': nrm(5, (L, 6 * D), 0.02),
        'norm_mix': 1.0 + nrm(6, (L, D), 0.05),
        'norm_mlp': 1.0 + nrm(7, (L, D), 0.05),
        'w_in': nrm(8, (L, D, N_IN), D ** -0.5),
        'conv_w': nrm(9, (L, SHORT_CONV, 3 * D_HYENA), SHORT_CONV ** -0.5),
        'conv_b': nrm(10, (L, 3 * D_HYENA), 0.02),
        'filt_w1': nrm(11, (L, FILTER_EMB, FILTER_HIDDEN), FILTER_EMB ** -0.5),
        'filt_b1': nrm(12, (L, FILTER_HIDDEN), 0.1),
        'filt_w2': nrm(13, (L, FILTER_HIDDEN, FILTER_HIDDEN), FILTER_HIDDEN ** -0.5),
        'filt_b2': nrm(14, (L, FILTER_HIDDEN), 0.1),
        'filt_w3': nrm(15, (L, FILTER_HIDDEN, FILTER_HIDDEN), FILTER_HIDDEN ** -0.5),
        'filt_b3': nrm(16, (L, FILTER_HIDDEN), 0.1),
        'filt_freq': 1.0 + nrm(17, (L, FILTER_HIDDEN), 0.1),
        'filt_wout': nrm(18, (L, FILTER_HIDDEN, 2 * D_HYENA), FILTER_HIDDEN ** -0.5),
        'hyena_bias': nrm(19, (L, D_HYENA), 0.5),
        'gk_w_fwd': nrm(20, (L, GATE_RANK, GLA_DK), GATE_RANK ** -0.5),
        'gk_b_fwd': nrm(21, (L, GLA_DK), 0.1),
        'gk_w_bwd': nrm(22, (L, GATE_RANK, GLA_DK), GATE_RANK ** -0.5),
        'gk_b_bwd': nrm(23, (L, GLA_DK), 0.1),
        'gla_norm': 1.0 + nrm(24, (L, GLA_HEAD_DV), 0.05),
        'w_out': nrm(25, (L, D, D), D ** -0.5),
        'w_mlp1': nrm(26, (L, D, D_FF), D ** -0.5),
        'w_mlp2': nrm(27, (L, D_FF, D), D_FF ** -0.5),
        'norm_final': 1.0 + nrm(28, (D,), 0.05),
    }


def reference(x, c, ctx, c_ctx, w_ada, b_ada, norm_mix, norm_mlp, w_in, conv_w, conv_b,
              filt_w1, filt_b1, filt_w2, filt_b2, filt_w3, filt_b3, filt_freq, filt_wout,
              hyena_bias, gk_w_fwd, gk_b_fwd, gk_w_bwd, gk_b_bwd, gla_norm, w_out,
              w_mlp1, w_mlp2, norm_final):
    bsz, seq, _ = x.shape
    rows = seq // GRID_W
    ctx_len = ctx.shape[1]
    for l in range(DEPTH):
        last = l == DEPTH - 1
        filt = (filt_w1[l], filt_b1[l], filt_w2[l], filt_b2[l], filt_w3[l], filt_b3[l], filt_freq[l], filt_wout[l])
        mod = jax.nn.silu(c) @ w_ada[l] + b_ada[l]
        mod_c = jax.nn.silu(c_ctx) @ w_ada[l] + b_ada[l]
        sh1, sc1, g1, sh2, sc2, g2 = jnp.split(mod[:, None, :], 6, axis=-1)
        csh1, csc1, cg1, csh2, csc2, cg2 = jnp.split(mod_c, 6)

        h_lat = _rms_norm(x, norm_mix[l]) * (1.0 + sc1) + sh1
        h_ctx = _rms_norm(ctx, norm_mix[l]) * (1.0 + csc1) + csh1
        hy_l, q_l, k_l, v_l, og_l, gf_l, gb_l = jnp.split(h_lat @ w_in[l], PROJ_SPLITS, axis=-1)
        hy_c, q_c, k_c, v_c, og_c, gf_c, gb_c = jnp.split(h_ctx @ w_in[l], PROJ_SPLITS, axis=-1)

        qh_c, kh_c, vh_c = _gla_heads(q_c, k_c, v_c)
        lgf_c = _log_gate(gf_c, gk_w_fwd[l], gk_b_fwd[l])
        lgb_c = _log_gate(gb_c, gk_w_bwd[l], gk_b_bwd[l])
        if last:
            s_f = _gla_final_state(kh_c, vh_c, lgf_c)
            s_b = _gla_final_state(_rev(kh_c), _rev(vh_c), _rev(lgb_c))
        else:
            zeros = jnp.zeros((bsz, GLA_HEADS, GLA_HEAD_DK, GLA_HEAD_DV), jnp.float32)
            o_c, s_f, s_b = _gla_bidir(qh_c, kh_c, vh_c, lgf_c, lgb_c, zeros, zeros)

        qh_l, kh_l, vh_l = _gla_heads(q_l, k_l, v_l)
        lgf_l = _log_gate(gf_l, gk_w_fwd[l], gk_b_fwd[l])
        lgb_l = _log_gate(gb_l, gk_w_bwd[l], gk_b_bwd[l])
        o_l, _, _ = _gla_bidir(qh_l, kh_l, vh_l, lgf_l, lgb_l, s_f, s_b)
        y_gla_l = _gla_output(o_l, og_l, gla_norm[l])

        u_l = _short_conv(hy_l.reshape(bsz, rows, GRID_W, 3 * D_HYENA), conv_w[l], conv_b[l])
        u_l = u_l.reshape(bsz, seq, 3 * D_HYENA)
        y_hy_l = _hyena_gate_conv(u_l, _hyena_spectrum(seq, *filt), hyena_bias[l])

        x = x + g1 * (jnp.concatenate([y_hy_l, y_gla_l], axis=-1) @ w_out[l])
        h2 = _rms_norm(x, norm_mlp[l]) * (1.0 + sc2) + sh2
        x = x + g2 * _sq_relu_mlp(h2, w_mlp1[l], w_mlp2[l])

        if not last:
            u_c = _short_conv(hy_c, conv_w[l], conv_b[l])
            y_hy_c = _hyena_gate_conv(u_c, _hyena_spectrum(ctx_len, *filt), hyena_bias[l])
            y_gla_c = _gla_output(o_c, og_c, gla_norm[l])
            ctx = ctx + cg1 * (jnp.concatenate([y_hy_c, y_gla_c], axis=-1) @ w_out[l])
            h2c = _rms_norm(ctx, norm_mlp[l]) * (1.0 + csc2) + csh2
            ctx = ctx + cg2 * _sq_relu_mlp(h2c, w_mlp1[l], w_mlp2[l])
    return _rms_norm(x, norm_final)
```

```python
import functools
import math

import jax
import jax.numpy as jnp
from jax import lax
from jax.experimental import pallas as pl
from jax.experimental.pallas import tpu as pltpu

D_MODEL = 1024
SEQ = 4096
GRID_W = 64
D_HYENA = 512
GLA_HEADS = 4
GLA_DK = 256
GLA_DV = 512
GLA_HEAD_DK = 64
GLA_HEAD_DV = 128
GATE_RANK = 16
GATE_NORMALIZER = 16.0
GLA_CHUNK = 64
FILTER_EMB = 33
FILTER_BANDS = 16
FILTER_HIDDEN = 64
FAST_DECAY_PCT = 0.3
SLOW_DECAY_PCT = 1.5
DECAY_TARGET = 1e-2
D_FF = 4 * D_MODEL
EPS = 1e-6
N_IN = 3104
N_IN_PAD = 3200
GATE_COL = 3072

FFT_N = 2 * SEQ
FFT_N1 = 32
FFT_N2 = 256
FFT_K1 = FFT_N1 // 2 + 1
CG = 256
CGF = 128
ROWS = 8

F32 = jnp.float32
BF16 = jnp.bfloat16
HI = lax.Precision.HIGHEST
VMEM_LIMIT = 56 * 1024 * 1024


def _dot(a, b, precision=None):
    return jnp.dot(a, b, preferred_element_type=F32, precision=precision)


def _add(a, b):
    if a is None:
        return b
    if b is None:
        return a
    return a + b


def _sub(a, b):
    if b is None:
        return a
    if a is None:
        return -b
    return a - b


def _scale(a, c):
    if a is None or c == 0.0:
        return None
    if c == 1.0:
        return a
    if c == -1.0:
        return -a
    return a * c


def _cadd(x, y):
    return (_add(x[0], y[0]), _add(x[1], y[1]))


def _csub(x, y):
    return (_sub(x[0], y[0]), _sub(x[1], y[1]))


def _conj(x):
    return (x[0], None if x[1] is None else -x[1])


def _snap(v):
    for t in (0.0, 1.0, -1.0):
        if abs(v - t) < 1e-12:
            return t
    return v


def _cmul_const(x, ang):
    c, s = _snap(math.cos(ang)), _snap(math.sin(ang))
    re = _sub(_scale(x[0], c), _scale(x[1], s))
    im = _add(_scale(x[0], s), _scale(x[1], c))
    return (re, im)


def _rfft_sym(x):
    n = len(x)
    if n == 1:
        return [(x[0], None)]
    if n == 2:
        return [(_add(x[0], x[1]), None), (_sub(x[0], x[1]), None)]
    ev = _rfft_sym(x[0::2])
    od = _rfft_sym(x[1::2])
    out = [None] * (n // 2 + 1)
    for k in range(n // 4 + 1):
        t = _cmul_const(od[k], -2.0 * math.pi * k / n)
        out[k] = _cadd(ev[k], t)
        if n // 2 - k != k:
            out[n // 2 - k] = _conj(_csub(ev[k], t))
    out[0] = (out[0][0], None)
    out[n // 2] = (out[n // 2][0], None)
    return out


def _irfft_sym(b, n, keep):
    if keep <= 0:
        return [None] * n
    if n == 1:
        return [b[0][0]]
    if n == 2:
        y1 = _sub(b[0][0], b[1][0]) if keep > 1 else None
        return [_add(b[0][0], b[1][0]), y1]
    b = list(b)
    b[0] = (b[0][0], None)
    b[n // 2] = (b[n // 2][0], None)
    ev, od = [], []
    for k in range(n // 4 + 1):
        cb = _conj(b[n // 2 - k])
        ev.append(_cadd(b[k], cb))
        od.append(_cmul_const(_csub(b[k], cb), 2.0 * math.pi * k / n))
    y = [None] * n
    y[0::2] = _irfft_sym(ev, n // 2, (keep + 1) // 2)
    y[1::2] = _irfft_sym(od, n // 2, keep // 2)
    return y


def _slow_forward(load_slab, n_slabs, a_ref, width):
    def body(i, carry):
        r = pl.multiple_of(i * ROWS, ROWS)
        xs = [load_slab(s, r) for s in range(n_slabs)] + [None] * (FFT_N1 - n_slabs)
        spec = _rfft_sym(xs)
        zero = jnp.zeros((ROWS, width), F32)
        for k1 in range(FFT_K1):
            re, im = spec[k1]
            a_ref[k1, pl.ds(r, ROWS), :] = zero if re is None else re
            a_ref[k1, pl.ds(FFT_N2 + r, ROWS), :] = zero if im is None else im
        return carry
    lax.fori_loop(0, FFT_N2 // ROWS, body, 0)


def _tile_lanes(t, width):
    reps = width // t.shape[-1]
    return t if reps == 1 else jnp.concatenate([t] * reps, axis=-1)


def _mod_kernel(c_ref, w_ref, b_ref, o_ref):
    c = c_ref[...]
    s = c * jax.nn.sigmoid(c)
    o_ref[...] = _dot(s, w_ref[...], HI) + b_ref[...]


def _mod_call(cc, w, b):
    n = w.shape[1]
    bn = 1024
    return pl.pallas_call(
        _mod_kernel,
        grid=(n // bn,),
        in_specs=[pl.BlockSpec(cc.shape, lambda j: (0, 0)),
                  pl.BlockSpec((D_MODEL, bn), lambda j: (0, j)),
                  pl.BlockSpec((1, bn), lambda j: (0, j))],
        out_specs=pl.BlockSpec((cc.shape[0], bn), lambda j: (0, j)),
        out_shape=jax.ShapeDtypeStruct((cc.shape[0], n), F32),
        compiler_params=pltpu.CompilerParams(vmem_limit_bytes=VMEM_LIMIT),
        name="mod",
    )(cc, w, b)


def _rms(x):
    return x * lax.rsqrt(jnp.mean(x * x, axis=-1, keepdims=True) + EPS)


def _in_proj_kernel(x_ref, mod_ref, g_ref, w_ref, gw_ref, gb_ref,
                    hy_ref, q_ref, k_ref, v_ref, og_ref, lgf_ref, lgb_ref):
    x = x_ref[0]
    sh = mod_ref[0, :, 0:D_MODEL]
    sc = mod_ref[0, :, D_MODEL:2 * D_MODEL]
    h = ((_rms(x) * g_ref[...]) * (1.0 + sc) + sh).astype(BF16)
    c0 = 3 * D_HYENA
    hy_ref[0] = _dot(h, w_ref[:, 0:c0]).astype(BF16)
    q = _dot(h, w_ref[:, c0:c0 + GLA_DK]) * (GLA_HEAD_DK ** -0.5)
    q_ref[0] = q.astype(BF16)
    c1 = c0 + GLA_DK
    k_ref[0] = _dot(h, w_ref[:, c1:c1 + GLA_DK]).astype(BF16)
    c2 = c1 + GLA_DK
    v_ref[0] = _dot(h, w_ref[:, c2:c2 + GLA_DV]).astype(BF16)
    c3 = c2 + GLA_DV
    og_ref[0] = _dot(h, w_ref[:, c3:c3 + GLA_DV]).astype(BF16)
    gr = _dot(h, w_ref[:, GATE_COL:N_IN_PAD])
    z = _dot(gr, gw_ref[...], HI) + gb_ref[...]
    lg = jax.nn.log_sigmoid(z) * (1.0 / GATE_NORMALIZER)
    lgf_ref[0] = lg[:, 0:GLA_DK]
    lgb_ref[0] = lg[:, GLA_DK:2 * GLA_DK]


def _in_proj_call(x, mod, g, w, gw, gb, tm):
    bsz, t, _ = x.shape
    def tile(width):
        return pl.BlockSpec((1, tm, width), lambda b, i: (b, i, 0))
    def whole(a):
        return pl.BlockSpec(a.shape, lambda b, i: (0,) * a.ndim)
    widths = (3 * D_HYENA, GLA_DK, GLA_DK, GLA_DV, GLA_DV, GLA_DK, GLA_DK)
    dtypes = (BF16, BF16, BF16, BF16, BF16, F32, F32)
    return pl.pallas_call(
        _in_proj_kernel,
        grid=(bsz, t // tm),
        in_specs=[tile(D_MODEL),
                  pl.BlockSpec((1, 1, 6 * D_MODEL), lambda b, i: (b, 0, 0)),
                  whole(g), whole(w), whole(gw), whole(gb)],
        out_specs=[tile(wd) for wd in widths],
        out_shape=[jax.ShapeDtypeStruct((bsz, t, wd), dt) for wd, dt in zip(widths, dtypes)],
        compiler_params=pltpu.CompilerParams(
            dimension_semantics=("parallel", "parallel"), vmem_limit_bytes=VMEM_LIMIT),
        name="in_proj",
    )(x, mod, g, w, gw, gb)


def _block_mask():
    r = lax.broadcasted_iota(jnp.int32, (GLA_DK, GLA_DV), 0) >> 6
    c = lax.broadcasted_iota(jnp.int32, (GLA_DK, GLA_DV), 1) >> 7
    return r == c


def _ctx_state_kernel(k_ref, v_ref, lgf_ref, lgb_ref, sf_ref, sb_ref):
    t = k_ref.shape[1]
    row = lax.broadcasted_iota(jnp.int32, (t, t), 0)
    col = lax.broadcasted_iota(jnp.int32, (t, t), 1)
    k = k_ref[0].astype(F32)
    v = v_ref[0]
    bm = _block_mask()
    for lg_ref, tri, o_ref in ((lgf_ref, col > row, sf_ref), (lgb_ref, col < row, sb_ref)):
        rem = _dot(tri.astype(F32), lg_ref[0], HI)
        kend = (k * jnp.exp(rem)).astype(BF16)
        ds = lax.dot_general(kend, v, (((0,), (0,)), ((), ())), preferred_element_type=F32)
        o_ref[0] = jnp.where(bm, ds, 0.0)


def _ctx_state_call(k, v, lgf, lgb):
    bsz, t, _ = k.shape
    def tile(width):
        return pl.BlockSpec((1, t, width), lambda b: (b, 0, 0))
    st = pl.BlockSpec((1, GLA_DK, GLA_DV), lambda b: (b, 0, 0))
    return pl.pallas_call(
        _ctx_state_kernel,
        grid=(bsz,),
        in_specs=[tile(GLA_DK), tile(GLA_DV), tile(GLA_DK), tile(GLA_DK)],
        out_specs=[st, st],
        out_shape=[jax.ShapeDtypeStruct((bsz, GLA_DK, GLA_DV), F32)] * 2,
        compiler_params=pltpu.CompilerParams(vmem_limit_bytes=VMEM_LIMIT),
        name="gla_ctx",
    )(k, v, lgf, lgb)


def _gla_chunk(q, k, v, lg, s, tri, tri_mask, bm, last_row):
    b = _dot(tri, lg, HI)
    total = b[last_row:last_row + 1]
    qin = q * jnp.exp(b)
    kin = (k * jnp.exp(-b)).astype(BF16)
    kend = (k * jnp.exp(total - b)).astype(BF16)
    outs = []
    for h in range(GLA_HEADS):
        lanes = lax.broadcasted_iota(jnp.int32, qin.shape, 1) >> 6
        qh = jnp.where(lanes == h, qin, 0.0).astype(BF16)
        att = lax.dot_general(qh, kin, (((1,), (1,)), ((), ())), preferred_element_type=F32)
        att = jnp.where(tri_mask, att, 0.0).astype(BF16)
        outs.append(_dot(att, v[:, h * GLA_HEAD_DV:(h + 1) * GLA_HEAD_DV]))
    o = jnp.concatenate(outs, axis=1) + _dot(qin.astype(BF16), s.astype(BF16))
    ones = jnp.ones((lg.shape[0], GLA_HEAD_DV), F32)
    dec = lax.dot_general(lg, ones, (((0,), (0,)), ((), ())),
                          preferred_element_type=F32, precision=HI)
    dec = _tile_lanes(jnp.exp(dec), GLA_DV)
    ds = lax.dot_general(kend, v, (((0,), (0,)), ((), ())), preferred_element_type=F32)
    s_new = dec * s + jnp.where(bm, ds, 0.0)
    return o, s_new


def _gla_kernel(*refs, reverse):
    if reverse:
        q_ref, k_ref, v_ref, lg_ref, s0_ref, of_ref, og_ref, gn_ref, o_ref, s_scr = refs
    else:
        q_ref, k_ref, v_ref, lg_ref, s0_ref, o_ref, s_scr = refs
    c = GLA_CHUNK

    @pl.when(pl.program_id(1) == 0)
    def _():
        s_scr[...] = s0_ref[0]

    row = lax.broadcasted_iota(jnp.int32, (c, c), 0)
    col = lax.broadcasted_iota(jnp.int32, (c, c), 1)
    tri_mask = (col >= row) if reverse else (col <= row)
    tri = tri_mask.astype(F32)
    bm = _block_mask()
    n_chunks = q_ref.shape[1] // c
    order = range(n_chunks - 1, -1, -1) if reverse else range(n_chunks)
    for ci in order:
        rows = slice(ci * c, (ci + 1) * c)
        o, s_new = _gla_chunk(q_ref[0, rows, :].astype(F32), k_ref[0, rows, :].astype(F32),
                              v_ref[0, rows, :], lg_ref[0, rows, :], s_scr[...],
                              tri, tri_mask, bm, 0 if reverse else c - 1)
        s_scr[...] = s_new
        if reverse:
            o = o + of_ref[0, rows, :]
            og = og_ref[0, rows, :].astype(F32)
            parts = []
            for h in range(GLA_HEADS):
                oh = o[:, h * GLA_HEAD_DV:(h + 1) * GLA_HEAD_DV]
                parts.append(_rms(oh) * gn_ref[...])
            y = jnp.concatenate(parts, axis=1) * (og * jax.nn.sigmoid(og))
            o_ref[0, rows, :] = y.astype(o_ref.dtype)
        else:
            o_ref[0, rows, :] = o


def _gla_call(q, k, v, lg, s0, extra, reverse, tt):
    bsz, t, _ = q.shape
    nt = t // tt
    if reverse:
        def tile(width):
            return pl.BlockSpec((1, tt, width), lambda b, j: (b, nt - 1 - j, 0))
    else:
        def tile(width):
            return pl.BlockSpec((1, tt, width), lambda b, j: (b, j, 0))
    in_specs = [tile(GLA_DK), tile(GLA_DK), tile(GLA_DV), tile(GLA_DK),
                pl.BlockSpec((1, GLA_DK, GLA_DV), lambda b, j: (b, 0, 0))]
    args = [q, k, v, lg, s0]
    if reverse:
        o_f, og, gn = extra
        in_specs += [tile(GLA_DV), tile(GLA_DV), pl.BlockSpec(gn.shape, lambda b, j: (0, 0))]
        args += [o_f, og, gn]
    return pl.pallas_call(
        functools.partial(_gla_kernel, reverse=reverse),
        grid=(bsz, nt),
        in_specs=in_specs,
        out_specs=tile(GLA_DV),
        out_shape=jax.ShapeDtypeStruct((bsz, t, GLA_DV), BF16 if reverse else F32),
        scratch_shapes=[pltpu.VMEM((GLA_DK, GLA_DV), F32)],
        compiler_params=pltpu.CompilerParams(
            dimension_semantics=("parallel", "arbitrary"), vmem_limit_bytes=VMEM_LIMIT),
        name="gla_bwd" if reverse else "gla_fwd",
    )(*args)


def _filter_mlp_kernel(z_ref, w1_ref, b1_ref, w2_ref, b2_ref, w3_ref, b3_ref, fr_ref, o_ref):
    fr = fr_ref[...]
    for d in range(2):
        h = jnp.sin(fr * (_dot(z_ref[d], w1_ref[...], HI) + b1_ref[...]))
        h = jnp.sin(fr * (_dot(h, w2_ref[...], HI) + b2_ref[...]))
        h = jnp.sin(fr * (_dot(h, w3_ref[...], HI) + b3_ref[...]))
        o_ref[d] = h


def _filter_mlp_call(z2, w1p, b1, w2, b2, w3, b3, fr):
    return pl.pallas_call(
        _filter_mlp_kernel,
        out_shape=jax.ShapeDtypeStruct((2, SEQ, FILTER_HIDDEN), F32),
        compiler_params=pltpu.CompilerParams(vmem_limit_bytes=VMEM_LIMIT),
        name="filter_mlp",
    )(z2, w1p, b1, w2, b2, w3, b3, fr)


def _filter_kernel(hdn_ref, wf_ref, wb_ref, t_ref, dl_ref, f2_ref, twr_ref, twi_ref,
                   khat_ref, kern_ref, a_ref):
    dl = dl_ref[...]
    blk = 512

    def taps_body(i, ss):
        r0 = pl.multiple_of(i * blk, blk)
        rows = pl.ds(r0, blk)
        hf = _dot(hdn_ref[0, rows, :], wf_ref[...], HI) * jnp.exp(-t_ref[0, rows, :] * dl)
        hb = _dot(hdn_ref[1, rows, :], wb_ref[...], HI) * jnp.exp(-t_ref[1, rows, :] * dl)
        lag = r0 + lax.broadcasted_iota(jnp.int32, hb.shape, 0)
        hb = jnp.where(lag == 0, 0.0, hb)
        kern_ref[rows, :] = hf
        kern_ref[pl.ds(SEQ + r0, blk), :] = hb
        return (ss + jnp.sum(hf * hf, axis=0, keepdims=True)
                + jnp.sum(hb * hb, axis=0, keepdims=True))
    ss = lax.fori_loop(0, SEQ // blk, taps_body, jnp.zeros((1, CGF), F32))
    scale = lax.rsqrt(ss + EPS) * (1.0 / FFT_N)

    def load_slab(s, r):
        return kern_ref[pl.ds(s * FFT_N2 + r, ROWS), :]
    _slow_forward(load_slab, FFT_N1, a_ref, CGF)

    def body(k1, carry):
        ar = a_ref[k1, 0:FFT_N2, :]
        ai = a_ref[k1, FFT_N2:2 * FFT_N2, :]
        twr = twr_ref[k1]
        twi = twi_ref[k1]
        a = jnp.concatenate([ar * twr - ai * twi, ar * twi + ai * twr], axis=0)
        khat_ref[k1] = _dot(f2_ref[...], a, HI) * scale
        return carry
    lax.fori_loop(0, FFT_K1, body, 0)


def _filter_call(hdn, wout, t2, deltas, f2, twr, twi):
    ng = D_HYENA // CGF
    def whole(a):
        return pl.BlockSpec(a.shape, lambda g: (0,) * a.ndim)
    return pl.pallas_call(
        _filter_kernel,
        grid=(ng,),
        in_specs=[whole(hdn),
                  pl.BlockSpec((FILTER_HIDDEN, CGF), lambda g: (0, g)),
                  pl.BlockSpec((FILTER_HIDDEN, CGF), lambda g: (0, ng + g)),
                  whole(t2),
                  pl.BlockSpec((1, CGF), lambda g: (0, g)),
                  whole(f2), whole(twr), whole(twi)],
        out_specs=pl.BlockSpec((FFT_K1, 2 * FFT_N2, CGF), lambda g: (0, 0, g)),
        out_shape=jax.ShapeDtypeStruct((FFT_K1, 2 * FFT_N2, D_HYENA), F32),
        scratch_shapes=[pltpu.VMEM((FFT_N, CGF), F32),
                        pltpu.VMEM((FFT_K1, 2 * FFT_N2, CGF), F32)],
        compiler_params=pltpu.CompilerParams(vmem_limit_bytes=VMEM_LIMIT),
        name="filter_spec",
    )(hdn, wout, wout, t2, deltas, f2, twr, twi)


def _short_conv(x, w_ref, b_ref):
    n = x.shape[0]
    pos = lax.broadcasted_iota(jnp.int32, x.shape, 0) & (GRID_W - 1)
    prev = jnp.where(pos == 0, 0.0, pltpu.roll(x, 1, 0))
    nxt = jnp.where(pos == GRID_W - 1, 0.0, pltpu.roll(x, n - 1, 0))
    return b_ref[...] + w_ref[0:1, :] * prev + w_ref[1:2, :] * x + w_ref[2:3, :] * nxt


def _hyena_kernel(x0_ref, x1_ref, v_ref, w0_ref, w1_ref, wv_ref, b0_ref, b1_ref, bv_ref,
                  bias_ref, khat_ref, ff_ref, fi_ref, twr_ref, twi_ref, o_ref, u_ref, a_ref):
    blk = FFT_N2

    def gate_body(i, carry):
        rows = pl.ds(pl.multiple_of(i * blk, blk), blk)
        x1 = _short_conv(x1_ref[0, rows, :].astype(F32), w1_ref, b1_ref)
        v = _short_conv(v_ref[0, rows, :].astype(F32), wv_ref, bv_ref)
        u_ref[rows, :] = x1 * v
        return carry
    lax.fori_loop(0, SEQ // blk, gate_body, 0)

    def load_slab(s, r):
        return u_ref[pl.ds(s * FFT_N2 + r, ROWS), :]
    _slow_forward(load_slab, FFT_N1 // 2, a_ref, CG)

    def freq_body(k1, carry):
        ar = a_ref[k1, 0:FFT_N2, :]
        ai = a_ref[k1, FFT_N2:2 * FFT_N2, :]
        twr = _tile_lanes(twr_ref[k1], CG)
        twi = _tile_lanes(twi_ref[k1], CG)
        a = jnp.concatenate([ar * twr - ai * twi, ar * twi + ai * twr], axis=0).astype(BF16)
        x = _dot(ff_ref[...], a)
        xr, xi = x[0:FFT_N2], x[FFT_N2:]
        kr = khat_ref[k1, 0:FFT_N2, :]
        ki = khat_ref[k1, FFT_N2:2 * FFT_N2, :]
        y = jnp.concatenate([xr * kr - xi * ki, xr * ki + xi * kr], axis=0).astype(BF16)
        bq = _dot(fi_ref[...], y)
        br, bi = bq[0:FFT_N2], bq[FFT_N2:]
        a_ref[k1, 0:FFT_N2, :] = br * twr + bi * twi
        a_ref[k1, FFT_N2:2 * FFT_N2, :] = bi * twr - br * twi
        return carry
    lax.fori_loop(0, FFT_K1, freq_body, 0)

    bias = bias_ref[...]

    def inv_body(i, carry):
        r = pl.multiple_of(i * ROWS, ROWS)
        spec = [(a_ref[k1, pl.ds(r, ROWS), :], a_ref[k1, pl.ds(FFT_N2 + r, ROWS), :])
                for k1 in range(FFT_K1)]
        ys = _irfft_sym(spec, FFT_N1, FFT_N1 // 2)
        for s in range(FFT_N1 // 2):
            rows = pl.ds(s * FFT_N2 + r, ROWS)
            u_ref[rows, :] = ys[s] + bias * u_ref[rows, :]
        return carry
    lax.fori_loop(0, FFT_N2 // ROWS, inv_body, 0)

    def out_body(i, carry):
        rows = pl.ds(pl.multiple_of(i * blk, blk), blk)
        x0 = _short_conv(x0_ref[0, rows, :].astype(F32), w0_ref, b0_ref)
        o_ref[0, rows, :] = (x0 * u_ref[rows, :]).astype(o_ref.dtype)
        return carry
    lax.fori_loop(0, SEQ // blk, out_body, 0)


def _hyena_call(hy, conv_w, conv_b, bias, khat, ff, fi, twr, twi):
    bsz = hy.shape[0]
    ng = D_HYENA // CG
    def stream(j):
        return pl.BlockSpec((1, SEQ, CG), lambda g, b: (b, 0, j * ng + g))
    def taps(j):
        return pl.BlockSpec((3, CG), lambda g, b: (0, j * ng + g))
    def offs(j):
        return pl.BlockSpec((1, CG), lambda g, b: (0, j * ng + g))
    def whole(a):
        return pl.BlockSpec(a.shape, lambda g, b: (0,) * a.ndim)
    return pl.pallas_call(
        _hyena_kernel,
        grid=(ng, bsz),
        in_specs=[stream(0), stream(1), stream(2), taps(0), taps(1), taps(2),
                  offs(0), offs(1), offs(2), offs(0),
                  pl.BlockSpec((FFT_K1, 2 * FFT_N2, CG), lambda g, b: (0, 0, g)),
                  whole(ff), whole(fi), whole(twr), whole(twi)],
        out_specs=pl.BlockSpec((1, SEQ, CG), lambda g, b: (b, 0, g)),
        out_shape=jax.ShapeDtypeStruct((bsz, SEQ, D_HYENA), BF16),
        scratch_shapes=[pltpu.VMEM((SEQ, CG), F32),
                        pltpu.VMEM((FFT_K1, 2 * FFT_N2, CG), F32)],
        compiler_params=pltpu.CompilerParams(
            dimension_semantics=("parallel", "parallel"), vmem_limit_bytes=VMEM_LIMIT),
        name="hyena",
    )(hy, hy, hy, conv_w, conv_w, conv_w, conv_b, conv_b, conv_b, bias, khat, ff, fi, twr, twi)


def _out_kernel(x_ref, yh_ref, yg_ref, mod_ref, wo_ref, nm_ref, w1_ref, w2_ref, nf_ref, o_ref):
    d = D_MODEL
    g1 = mod_ref[0, :, 2 * d:3 * d]
    sh2 = mod_ref[0, :, 3 * d:4 * d]
    sc2 = mod_ref[0, :, 4 * d:5 * d]
    g2 = mod_ref[0, :, 5 * d:6 * d]
    mix = _dot(yh_ref[0], wo_ref[0:D_HYENA, :]) + _dot(yg_ref[0], wo_ref[D_HYENA:d, :])
    x1 = x_ref[0] + g1 * mix
    h2 = ((_rms(x1) * nm_ref[...]) * (1.0 + sc2) + sh2).astype(BF16)
    acc = jnp.zeros(x1.shape, F32)
    for j in range(D_FF // d):
        hid = _dot(h2, w1_ref[:, j * d:(j + 1) * d])
        hid = jnp.square(jnp.maximum(hid, 0.0)).astype(BF16)
        acc = acc + _dot(hid, w2_ref[j * d:(j + 1) * d, :])
    x2 = x1 + g2 * acc
    o_ref[0] = _rms(x2) * nf_ref[...]


def _out_call(x, yh, yg, mod, wo, nm, w1, w2, nf, tm):
    bsz, t, _ = x.shape
    def tile(width):
        return pl.BlockSpec((1, tm, width), lambda b, i: (b, i, 0))
    def whole(a):
        return pl.BlockSpec(a.shape, lambda b, i: (0,) * a.ndim)
    return pl.pallas_call(
        _out_kernel,
        grid=(bsz, t // tm),
        in_specs=[tile(D_MODEL), tile(D_HYENA), tile(GLA_DV),
                  pl.BlockSpec((1, 1, 6 * D_MODEL), lambda b, i: (b, 0, 0)),
                  whole(wo), whole(nm), whole(w1), whole(w2), whole(nf)],
        out_specs=tile(D_MODEL),
        out_shape=jax.ShapeDtypeStruct((bsz, t, D_MODEL), F32),
        compiler_params=pltpu.CompilerParams(
            dimension_semantics=("parallel", "parallel"), vmem_limit_bytes=VMEM_LIMIT),
        name="out_mlp",
    )(x, yh, yg, mod, wo, nm, w1, w2, nf)


def _filter_features():
    length = SEQ
    t = jnp.linspace(0.0, 1.0, length, dtype=F32)[:, None]
    w = 2.0 * math.pi * jnp.arange(length, dtype=F32) / length
    f = jnp.linspace(1e-4, FILTER_BANDS - 1, FILTER_BANDS, dtype=F32)
    ang = w[:, None] * f[None, :]
    z = jnp.concatenate([t, jnp.cos(ang), -jnp.sin(ang)], axis=-1)
    def back(a):
        return jnp.concatenate([a[0:1], a[:0:-1]], axis=0)
    z2 = jnp.stack([z, back(z)])
    z2 = jnp.pad(z2, ((0, 0), (0, 0), (0, 128 - FILTER_EMB)))
    t2 = jnp.stack([t, back(t)])
    t2 = jnp.broadcast_to(t2, (2, length, 128))
    min_decay = math.log(DECAY_TARGET) / SLOW_DECAY_PCT
    max_decay = math.log(DECAY_TARGET) / FAST_DECAY_PCT
    deltas = jnp.abs(jnp.linspace(min_decay, max_decay, D_HYENA, dtype=F32))[None, :]
    return z2, t2, deltas


def _dft_tables():
    n2 = jnp.arange(FFT_N2, dtype=jnp.int32)
    k1 = jnp.arange(FFT_K1, dtype=jnp.int32)
    ang_tw = (2.0 * math.pi / FFT_N) * (k1[:, None] * n2[None, :]).astype(F32)
    twr = jnp.broadcast_to(jnp.cos(ang_tw)[:, :, None], (FFT_K1, FFT_N2, 128))
    twi = jnp.broadcast_to(-jnp.sin(ang_tw)[:, :, None], (FFT_K1, FFT_N2, 128))
    ang = (2.0 * math.pi / FFT_N2) * ((n2[:, None] * n2[None, :]) % FFT_N2).astype(F32)
    cr, ci = jnp.cos(ang), -jnp.sin(ang)
    fwd = jnp.block([[cr, -ci], [ci, cr]])
    inv = jnp.block([[cr, ci], [-ci, cr]])
    return fwd, inv, twr, twi


def kernel(x, c, ctx, c_ctx, w_ada, b_ada, norm_mix, norm_mlp, w_in, conv_w, conv_b, filt_w1, filt_b1, filt_w2, filt_b2, filt_w3, filt_b3, filt_freq, filt_wout, hyena_bias, gk_w_fwd, gk_b_fwd, gk_w_bwd, gk_b_bwd, gla_norm, w_out, w_mlp1, w_mlp2, norm_final):
    bsz = x.shape[0]
    l = 0
    cc = jnp.concatenate([c, c_ctx[None, :], jnp.zeros((16 - bsz - 1, D_MODEL), F32)], axis=0)
    mod_all = _mod_call(cc, w_ada[l], b_ada[l][None, :])
    mod = mod_all[:bsz][:, None, :]
    mod_c = jnp.broadcast_to(mod_all[bsz][None, None, :], (bsz, 1, 6 * D_MODEL))

    w_in_p = jnp.pad(w_in[l], ((0, 0), (0, N_IN_PAD - N_IN))).astype(BF16)
    gw = jnp.zeros((N_IN_PAD - GATE_COL, 2 * GLA_DK), F32)
    gw = gw.at[0:GATE_RANK, 0:GLA_DK].set(gk_w_fwd[l])
    gw = gw.at[GATE_RANK:2 * GATE_RANK, GLA_DK:].set(gk_w_bwd[l])
    gb = jnp.concatenate([gk_b_fwd[l], gk_b_bwd[l]])[None, :]
    g_mix = norm_mix[l][None, :]

    hy, q, k, v, og, lgf, lgb = _in_proj_call(x, mod, g_mix, w_in_p, gw, gb, 512)
    _, _, k_c, v_c, _, lgf_c, lgb_c = _in_proj_call(ctx, mod_c, g_mix, w_in_p, gw, gb, ctx.shape[1])

    s_f, s_b = _ctx_state_call(k_c, v_c, lgf_c, lgb_c)
    o_f = _gla_call(q, k, v, lgf, s_f, None, False, 512)
    y_gla = _gla_call(q, k, v, lgb, s_b, (o_f, og, gla_norm[l][None, :]), True, 512)

    z2, t2, deltas = _filter_features()
    f2_fwd, f2_inv, twr, twi = _dft_tables()
    w1p = jnp.pad(filt_w1[l], ((0, 128 - FILTER_EMB), (0, 0)))
    hdn = _filter_mlp_call(z2, w1p, filt_b1[l][None, :], filt_w2[l], filt_b2[l][None, :],
                           filt_w3[l], filt_b3[l][None, :], filt_freq[l][None, :])
    khat = _filter_call(hdn, filt_wout[l], t2, deltas, f2_fwd, twr, twi)
    y_hy = _hyena_call(hy, conv_w[l], conv_b[l][None, :], hyena_bias[l][None, :], khat,
                       f2_fwd.astype(BF16), f2_inv.astype(BF16), twr, twi)

    return _out_call(x, y_hy, y_gla, mod, w_out[l].astype(BF16), norm_mlp[l][None, :],
                     w_mlp1[l].astype(BF16), w_mlp2[l].astype(BF16), norm_final[None, :], 512)
```

```python
import functools
import math

import jax
import jax.numpy as jnp
from jax import lax
from jax.experimental import pallas as pl
from jax.experimental.pallas import tpu as pltpu

D_MODEL = 1024
SEQ = 4096
GRID_W = 64
D_HYENA = 512
GLA_HEADS = 4
GLA_DK = 256
GLA_DV = 512
GLA_HEAD_DK = 64
GLA_HEAD_DV = 128
GATE_RANK = 16
GATE_NORMALIZER = 16.0
GLA_CHUNK = 64
FILTER_EMB = 33
FILTER_BANDS = 16
FILTER_HIDDEN = 64
FAST_DECAY_PCT = 0.3
SLOW_DECAY_PCT = 1.5
DECAY_TARGET = 1e-2
D_FF = 4 * D_MODEL
EPS = 1e-6
N_IN = 3104
N_IN_PAD = 3200
GATE_COL = 3072

FFT_N = 2 * SEQ
FFT_N1 = 32
FFT_N2 = 256
FFT_K1 = FFT_N1 // 2 + 1
CG = 256
CGF = 128
ROWS = 8

F32 = jnp.float32
BF16 = jnp.bfloat16
HI = lax.Precision.HIGHEST
VMEM_LIMIT = 56 * 1024 * 1024


def _dot(a, b, precision=None):
    return jnp.dot(a, b, preferred_element_type=F32, precision=precision)


def _add(a, b):
    if a is None:
        return b
    if b is None:
        return a
    return a + b


def _sub(a, b):
    if b is None:
        return a
    if a is None:
        return -b
    return a - b


def _scale(a, c):
    if a is None or c == 0.0:
        return None
    if c == 1.0:
        return a
    if c == -1.0:
        return -a
    return a * c


def _cadd(x, y):
    return (_add(x[0], y[0]), _add(x[1], y[1]))


def _csub(x, y):
    return (_sub(x[0], y[0]), _sub(x[1], y[1]))


def _conj(x):
    return (x[0], None if x[1] is None else -x[1])


def _snap(v):
    for t in (0.0, 1.0, -1.0):
        if abs(v - t) < 1e-12:
            return t
    return v


def _cmul_const(x, ang):
    c, s = _snap(math.cos(ang)), _snap(math.sin(ang))
    re = _sub(_scale(x[0], c), _scale(x[1], s))
    im = _add(_scale(x[0], s), _scale(x[1], c))
    return (re, im)


def _rfft_sym(x):
    n = len(x)
    if n == 1:
        return [(x[0], None)]
    if n == 2:
        return [(_add(x[0], x[1]), None), (_sub(x[0], x[1]), None)]
    ev = _rfft_sym(x[0::2])
    od = _rfft_sym(x[1::2])
    out = [None] * (n // 2 + 1)
    for k in range(n // 4 + 1):
        t = _cmul_const(od[k], -2.0 * math.pi * k / n)
        out[k] = _cadd(ev[k], t)
        if n // 2 - k != k:
            out[n // 2 - k] = _conj(_csub(ev[k], t))
    out[0] = (out[0][0], None)
    out[n // 2] = (out[n // 2][0], None)
    return out


def _irfft_sym(b, n, keep):
    if keep <= 0:
        return [None] * n
    if n == 1:
        return [b[0][0]]
    if n == 2:
        y1 = _sub(b[0][0], b[1][0]) if keep > 1 else None
        return [_add(b[0][0], b[1][0]), y1]
    b = list(b)
    b[0] = (b[0][0], None)
    b[n // 2] = (b[n // 2][0], None)
    ev, od = [], []
    for k in range(n // 4 + 1):
        cb = _conj(b[n // 2 - k])
        ev.append(_cadd(b[k], cb))
        od.append(_cmul_const(_csub(b[k], cb), 2.0 * math.pi * k / n))
    y = [None] * n
    y[0::2] = _irfft_sym(ev, n // 2, (keep + 1) // 2)
    y[1::2] = _irfft_sym(od, n // 2, keep // 2)
    return y


def _slow_forward(load_slab, n_slabs, a_ref, width):
    def body(i, carry):
        r = pl.multiple_of(i * ROWS, ROWS)
        xs = [load_slab(s, r) for s in range(n_slabs)] + [None] * (FFT_N1 - n_slabs)
        spec = _rfft_sym(xs)
        zero = jnp.zeros((ROWS, width), F32)
        for k1 in range(FFT_K1):
            re, im = spec[k1]
            a_ref[k1, pl.ds(r, ROWS), :] = zero if re is None else re
            a_ref[k1, pl.ds(FFT_N2 + r, ROWS), :] = zero if im is None else im
        return carry
    lax.fori_loop(0, FFT_N2 // ROWS, body, 0)


def _tile_lanes(t, width):
    reps = width // t.shape[-1]
    return t if reps == 1 else jnp.concatenate([t] * reps, axis=-1)


def _mod_kernel(c_ref, w_ref, b_ref, o_ref):
    c = c_ref[...]
    s = c * jax.nn.sigmoid(c)
    o_ref[...] = _dot(s, w_ref[...], HI) + b_ref[...]


def _mod_call(cc, w, b):
    n = w.shape[1]
    bn = 1024
    return pl.pallas_call(
        _mod_kernel,
        grid=(n // bn,),
        in_specs=[pl.BlockSpec(cc.shape, lambda j: (0, 0)),
                  pl.BlockSpec((D_MODEL, bn), lambda j: (0, j)),
                  pl.BlockSpec((1, bn), lambda j: (0, j))],
        out_specs=pl.BlockSpec((cc.shape[0], bn), lambda j: (0, j)),
        out_shape=jax.ShapeDtypeStruct((cc.shape[0], n), F32),
        compiler_params=pltpu.CompilerParams(vmem_limit_bytes=VMEM_LIMIT),
        name="mod",
    )(cc, w, b)


def _rms(x):
    return x * lax.rsqrt(jnp.mean(x * x, axis=-1, keepdims=True) + EPS)


def _in_proj_kernel(x_ref, mod_ref, g_ref, w_ref, gw_ref, gb_ref,
                    hy_ref, q_ref, k_ref, v_ref, og_ref, lgf_ref, lgb_ref):
    x = x_ref[0]
    sh = mod_ref[0, :, 0:D_MODEL]
    sc = mod_ref[0, :, D_MODEL:2 * D_MODEL]
    h = ((_rms(x) * g_ref[...]) * (1.0 + sc) + sh).astype(BF16)
    c0 = 3 * D_HYENA
    hy_ref[0] = _dot(h, w_ref[:, 0:c0]).astype(BF16)
    q = _dot(h, w_ref[:, c0:c0 + GLA_DK]) * (GLA_HEAD_DK ** -0.5)
    q_ref[0] = q.astype(BF16)
    c1 = c0 + GLA_DK
    k_ref[0] = _dot(h, w_ref[:, c1:c1 + GLA_DK]).astype(BF16)
    c2 = c1 + GLA_DK
    v_ref[0] = _dot(h, w_ref[:, c2:c2 + GLA_DV]).astype(BF16)
    c3 = c2 + GLA_DV
    og_ref[0] = _dot(h, w_ref[:, c3:c3 + GLA_DV]).astype(BF16)
    gr = _dot(h, w_ref[:, GATE_COL:N_IN_PAD])
    z = _dot(gr, gw_ref[...], HI) + gb_ref[...]
    lg = jax.nn.log_sigmoid(z) * (1.0 / GATE_NORMALIZER)
    lgf_ref[0] = lg[:, 0:GLA_DK]
    lgb_ref[0] = lg[:, GLA_DK:2 * GLA_DK]


def _in_proj_call(x, mod, g, w, gw, gb, tm):
    bsz, t, _ = x.shape
    def tile(width):
        return pl.BlockSpec((1, tm, width), lambda b, i: (b, i, 0))
    def whole(a):
        return pl.BlockSpec(a.shape, lambda b, i: (0,) * a.ndim)
    widths = (3 * D_HYENA, GLA_DK, GLA_DK, GLA_DV, GLA_DV, GLA_DK, GLA_DK)
    dtypes = (BF16, BF16, BF16, BF16, BF16, F32, F32)
    return pl.pallas_call(
        _in_proj_kernel,
        grid=(bsz, t // tm),
        in_specs=[tile(D_MODEL),
                  pl.BlockSpec((1, 1, 6 * D_MODEL), lambda b, i: (b, 0, 0)),
                  whole(g), whole(w), whole(gw), whole(gb)],
        out_specs=[tile(wd) for wd in widths],
        out_shape=[jax.ShapeDtypeStruct((bsz, t, wd), dt) for wd, dt in zip(widths, dtypes)],
        compiler_params=pltpu.CompilerParams(
            dimension_semantics=("parallel", "parallel"), vmem_limit_bytes=VMEM_LIMIT),
        name="in_proj",
    )(x, mod, g, w, gw, gb)


def _ctx_state_kernel(k_ref, v_ref, lgf_ref, lgb_ref, sf_ref, sb_ref):
    t = k_ref.shape[1]
    row = lax.broadcasted_iota(jnp.int32, (t, t), 0)
    col = lax.broadcasted_iota(jnp.int32, (t, t), 1)
    k = k_ref[0].astype(F32)
    v = v_ref[0]
    for lg_ref, tri, o_ref in ((lgf_ref, col > row, sf_ref), (lgb_ref, col < row, sb_ref)):
        rem = _dot(tri.astype(F32), lg_ref[0], HI)
        kend = (k * jnp.exp(rem)).astype(BF16)
        ds = lax.dot_general(kend, v, (((0,), (0,)), ((), ())), preferred_element_type=F32)
        for h in range(GLA_HEADS):
            o_ref[0, h] = ds[h * GLA_HEAD_DK:(h + 1) * GLA_HEAD_DK,
                             h * GLA_HEAD_DV:(h + 1) * GLA_HEAD_DV]


def _ctx_state_call(k, v, lgf, lgb):
    bsz, t, _ = k.shape
    def tile(width):
        return pl.BlockSpec((1, t, width), lambda b: (b, 0, 0))
    st = pl.BlockSpec((1, GLA_HEADS, GLA_HEAD_DK, GLA_HEAD_DV), lambda b: (b, 0, 0, 0))
    return pl.pallas_call(
        _ctx_state_kernel,
        grid=(bsz,),
        in_specs=[tile(GLA_DK), tile(GLA_DV), tile(GLA_DK), tile(GLA_DK)],
        out_specs=[st, st],
        out_shape=[jax.ShapeDtypeStruct((bsz, GLA_HEADS, GLA_HEAD_DK, GLA_HEAD_DV), F32)] * 2,
        compiler_params=pltpu.CompilerParams(vmem_limit_bytes=VMEM_LIMIT),
        name="gla_ctx",
    )(k, v, lgf, lgb)


def _prefix_rows(x, reverse):
    n = x.shape[0]
    pos = lax.broadcasted_iota(jnp.int32, x.shape, 0) & (GLA_CHUNK - 1)
    k = 1
    while k < GLA_CHUNK:
        if reverse:
            x = x + jnp.where(pos < GLA_CHUNK - k, pltpu.roll(x, n - k, 0), 0.0)
        else:
            x = x + jnp.where(pos >= k, pltpu.roll(x, k, 0), 0.0)
        k *= 2
    return x


def _pair_diag(a, b):
    z = jnp.zeros_like(a)
    return jnp.concatenate([jnp.concatenate([a, z], axis=1),
                            jnp.concatenate([z, b], axis=1)], axis=0)


def _gla_chunk(q, k, v, b, s, reverse):
    c, hd, hv = GLA_CHUNK, GLA_HEAD_DK, GLA_HEAD_DV
    total = b[0:1] if reverse else b[c - 1:c]
    qin = (q * jnp.exp(b)).astype(BF16)
    kin = k * jnp.exp(-b)
    kend = k * jnp.exp(total - b)
    dec = jnp.exp(total)
    hi = dec.astype(BF16).astype(F32)
    mid = (dec - hi).astype(BF16).astype(F32)
    lo = dec - hi - mid
    r8 = lax.broadcasted_iota(jnp.int32, (8, GLA_DK), 0)
    extra = jnp.where(r8 == 0, hi, jnp.where(r8 == 1, mid, jnp.where(r8 == 2, lo, 0.0)))
    a = jnp.concatenate([kend, extra, jnp.zeros((c - 8, GLA_DK), F32)], axis=0)
    at = a.T.astype(BF16)
    r64 = lax.broadcasted_iota(jnp.int32, (c, 2 * hv), 0)
    l64 = lax.broadcasted_iota(jnp.int32, (c, 2 * hv), 1)
    ones_blk = jnp.where((r64 < 3) & (l64 >= hv), 1.0, 0.0).astype(BF16)
    zv = jnp.zeros((c, hv), BF16)
    key = lax.broadcasted_iota(jnp.int32, (c, 2 * hd), 1)
    qrow = lax.broadcasted_iota(jnp.int32, (c, 2 * hd), 0)
    att_mask = ((key & (hd - 1)) >= qrow) if reverse else ((key & (hd - 1)) <= qrow)
    outs, s_new = [], list(s)
    for p in range(GLA_HEADS // 2):
        lanes = slice(p * 2 * hd, (p + 1) * 2 * hd)
        q2, k2 = qin[:, lanes], kin[:, lanes]
        kbd = jnp.concatenate([jnp.where(key < hd, k2, 0.0), jnp.where(key >= hd, k2, 0.0)],
                              axis=0).astype(BF16)
        att = lax.dot_general(q2, kbd, (((1,), (1,)), ((), ())), preferred_element_type=F32)
        att = jnp.where(att_mask, att, 0.0).astype(BF16)
        h0, h1 = 2 * p, 2 * p + 1
        v0, v1 = v[:, h0 * hv:(h0 + 1) * hv], v[:, h1 * hv:(h1 + 1) * hv]
        rhs = jnp.concatenate([_pair_diag(v0, v1),
                               _pair_diag(s[h0].astype(BF16), s[h1].astype(BF16))], axis=0)
        outs.append(_dot(jnp.concatenate([att, q2], axis=1), rhs))
        for h, vh in ((h0, v0), (h1, v1)):
            inc = _dot(at[h * hd:(h + 1) * hd, :],
                       jnp.concatenate([jnp.concatenate([vh, zv], axis=1), ones_blk], axis=0))
            s_new[h] = inc[:, hv:] * s[h] + inc[:, :hv]
    return jnp.concatenate(outs, axis=1), s_new


def _gla_kernel(*refs, reverse):
    if reverse:
        q_ref, k_ref, v_ref, lg_ref, s0_ref, of_ref, og_ref, gn_ref, o_ref, s_scr, b_scr = refs
    else:
        q_ref, k_ref, v_ref, lg_ref, s0_ref, o_ref, s_scr, b_scr = refs
    c = GLA_CHUNK

    @pl.when(pl.program_id(1) == 0)
    def _():
        s_scr[...] = s0_ref[0]

    b_scr[...] = _prefix_rows(lg_ref[0], reverse)
    s = [s_scr[h] for h in range(GLA_HEADS)]
    n_chunks = q_ref.shape[1] // c
    order = range(n_chunks - 1, -1, -1) if reverse else range(n_chunks)
    for ci in order:
        rows = slice(ci * c, (ci + 1) * c)
        o, s = _gla_chunk(q_ref[0, rows, :].astype(F32), k_ref[0, rows, :].astype(F32),
                          v_ref[0, rows, :], b_scr[rows, :], s, reverse)
        if reverse:
            o = o + of_ref[0, rows, :]
            og = og_ref[0, rows, :].astype(F32)
            parts = []
            for h in range(GLA_HEADS):
                oh = o[:, h * GLA_HEAD_DV:(h + 1) * GLA_HEAD_DV]
                parts.append(_rms(oh) * gn_ref[...])
            y = jnp.concatenate(parts, axis=1) * (og * jax.nn.sigmoid(og))
            o_ref[0, rows, :] = y.astype(o_ref.dtype)
        else:
            o_ref[0, rows, :] = o
    for h in range(GLA_HEADS):
        s_scr[h] = s[h]


def _gla_call(q, k, v, lg, s0, extra, reverse, tt):
    bsz, t, _ = q.shape
    nt = t // tt
    if reverse:
        def tile(width):
            return pl.BlockSpec((1, tt, width), lambda b, j: (b, nt - 1 - j, 0))
    else:
        def tile(width):
            return pl.BlockSpec((1, tt, width), lambda b, j: (b, j, 0))
    in_specs = [tile(GLA_DK), tile(GLA_DK), tile(GLA_DV), tile(GLA_DK),
                pl.BlockSpec((1, GLA_HEADS, GLA_HEAD_DK, GLA_HEAD_DV), lambda b, j: (b, 0, 0, 0))]
    args = [q, k, v, lg, s0]
    if reverse:
        o_f, og, gn = extra
        in_specs += [tile(GLA_DV), tile(GLA_DV), pl.BlockSpec(gn.shape, lambda b, j: (0, 0))]
        args += [o_f, og, gn]
    return pl.pallas_call(
        functools.partial(_gla_kernel, reverse=reverse),
        grid=(bsz, nt),
        in_specs=in_specs,
        out_specs=tile(GLA_DV),
        out_shape=jax.ShapeDtypeStruct((bsz, t, GLA_DV), BF16 if reverse else F32),
        scratch_shapes=[pltpu.VMEM((GLA_HEADS, GLA_HEAD_DK, GLA_HEAD_DV), F32),
                        pltpu.VMEM((tt, GLA_DK), F32)],
        compiler_params=pltpu.CompilerParams(
            dimension_semantics=("parallel", "arbitrary"), vmem_limit_bytes=VMEM_LIMIT),
        name="gla_bwd" if reverse else "gla_fwd",
    )(*args)


def _filter_mlp_kernel(z_ref, w1_ref, b1_ref, w2_ref, b2_ref, w3_ref, b3_ref, fr_ref, o_ref):
    fr = fr_ref[...]
    for d in range(2):
        h = jnp.sin(fr * (_dot(z_ref[d], w1_ref[...], HI) + b1_ref[...]))
        h = jnp.sin(fr * (_dot(h, w2_ref[...], HI) + b2_ref[...]))
        h = jnp.sin(fr * (_dot(h, w3_ref[...], HI) + b3_ref[...]))
        o_ref[d] = h


def _filter_mlp_call(z2, w1p, b1, w2, b2, w3, b3, fr):
    return pl.pallas_call(
        _filter_mlp_kernel,
        out_shape=jax.ShapeDtypeStruct((2, SEQ, FILTER_HIDDEN), F32),
        compiler_params=pltpu.CompilerParams(vmem_limit_bytes=VMEM_LIMIT),
        name="filter_mlp",
    )(z2, w1p, b1, w2, b2, w3, b3, fr)


def _filter_kernel(hdn_ref, wf_ref, wb_ref, t_ref, dl_ref, f2_ref, twr_ref, twi_ref,
                   khat_ref, kern_ref, a_ref):
    dl = dl_ref[...]
    blk = 512

    def taps_body(i, ss):
        r0 = pl.multiple_of(i * blk, blk)
        rows = pl.ds(r0, blk)
        hf = _dot(hdn_ref[0, rows, :], wf_ref[...], HI) * jnp.exp(-t_ref[0, rows, :] * dl)
        hb = _dot(hdn_ref[1, rows, :], wb_ref[...], HI) * jnp.exp(-t_ref[1, rows, :] * dl)
        lag = r0 + lax.broadcasted_iota(jnp.int32, hb.shape, 0)
        hb = jnp.where(lag == 0, 0.0, hb)
        kern_ref[rows, :] = hf
        kern_ref[pl.ds(SEQ + r0, blk), :] = hb
        return (ss + jnp.sum(hf * hf, axis=0, keepdims=True)
                + jnp.sum(hb * hb, axis=0, keepdims=True))
    ss = lax.fori_loop(0, SEQ // blk, taps_body, jnp.zeros((1, CGF), F32))
    scale = lax.rsqrt(ss + EPS) * (1.0 / FFT_N)

    def load_slab(s, r):
        return kern_ref[pl.ds(s * FFT_N2 + r, ROWS), :]
    _slow_forward(load_slab, FFT_N1, a_ref, CGF)

    def body(k1, carry):
        ar = a_ref[k1, 0:FFT_N2, :]
        ai = a_ref[k1, FFT_N2:2 * FFT_N2, :]
        twr = twr_ref[k1]
        twi = twi_ref[k1]
        a = jnp.concatenate([ar * twr - ai * twi, ar * twi + ai * twr], axis=0)
        khat_ref[k1] = _dot(f2_ref[...], a, HI) * scale
        return carry
    lax.fori_loop(0, FFT_K1, body, 0)


def _filter_call(hdn, wout, t2, deltas, f2, twr, twi):
    ng = D_HYENA // CGF
    def whole(a):
        return pl.BlockSpec(a.shape, lambda g: (0,) * a.ndim)
    return pl.pallas_call(
        _filter_kernel,
        grid=(ng,),
        in_specs=[whole(hdn),
                  pl.BlockSpec((FILTER_HIDDEN, CGF), lambda g: (0, g)),
                  pl.BlockSpec((FILTER_HIDDEN, CGF), lambda g: (0, ng + g)),
                  whole(t2),
                  pl.BlockSpec((1, CGF), lambda g: (0, g)),
                  whole(f2), whole(twr), whole(twi)],
        out_specs=pl.BlockSpec((FFT_K1, 2 * FFT_N2, CGF), lambda g: (0, 0, g)),
        out_shape=jax.ShapeDtypeStruct((FFT_K1, 2 * FFT_N2, D_HYENA), F32),
        scratch_shapes=[pltpu.VMEM((FFT_N, CGF), F32),
                        pltpu.VMEM((FFT_K1, 2 * FFT_N2, CGF), F32)],
        compiler_params=pltpu.CompilerParams(vmem_limit_bytes=VMEM_LIMIT),
        name="filter_spec",
    )(hdn, wout, wout, t2, deltas, f2, twr, twi)


def _short_conv(x, w_ref, b_ref):
    n = x.shape[0]
    pos = lax.broadcasted_iota(jnp.int32, x.shape, 0) & (GRID_W - 1)
    prev = jnp.where(pos == 0, 0.0, pltpu.roll(x, 1, 0))
    nxt = jnp.where(pos == GRID_W - 1, 0.0, pltpu.roll(x, n - 1, 0))
    return b_ref[...] + w_ref[0:1, :] * prev + w_ref[1:2, :] * x + w_ref[2:3, :] * nxt


def _hyena_kernel(x0_ref, x1_ref, v_ref, w0_ref, w1_ref, wv_ref, b0_ref, b1_ref, bv_ref,
                  bias_ref, khat_ref, ff_ref, fi_ref, twr_ref, twi_ref, o_ref, u_ref, a_ref):
    blk = FFT_N2

    def gate_body(i, carry):
        rows = pl.ds(pl.multiple_of(i * blk, blk), blk)
        x1 = _short_conv(x1_ref[0, rows, :].astype(F32), w1_ref, b1_ref)
        v = _short_conv(v_ref[0, rows, :].astype(F32), wv_ref, bv_ref)
        u_ref[rows, :] = x1 * v
        return carry
    lax.fori_loop(0, SEQ // blk, gate_body, 0)

    def load_slab(s, r):
        return u_ref[pl.ds(s * FFT_N2 + r, ROWS), :]
    _slow_forward(load_slab, FFT_N1 // 2, a_ref, CG)

    def freq_body(k1, carry):
        ar = a_ref[k1, 0:FFT_N2, :]
        ai = a_ref[k1, FFT_N2:2 * FFT_N2, :]
        twr = _tile_lanes(twr_ref[k1], CG)
        twi = _tile_lanes(twi_ref[k1], CG)
        a = jnp.concatenate([ar * twr - ai * twi, ar * twi + ai * twr], axis=0).astype(BF16)
        x = _dot(ff_ref[...], a)
        xr, xi = x[0:FFT_N2], x[FFT_N2:]
        kr = khat_ref[k1, 0:FFT_N2, :]
        ki = khat_ref[k1, FFT_N2:2 * FFT_N2, :]
        y = jnp.concatenate([xr * kr - xi * ki, xr * ki + xi * kr], axis=0).astype(BF16)
        bq = _dot(fi_ref[...], y)
        br, bi = bq[0:FFT_N2], bq[FFT_N2:]
        a_ref[k1, 0:FFT_N2, :] = br * twr + bi * twi
        a_ref[k1, FFT_N2:2 * FFT_N2, :] = bi * twr - br * twi
        return carry
    lax.fori_loop(0, FFT_K1, freq_body, 0)

    bias = bias_ref[...]

    def inv_body(i, carry):
        r = pl.multiple_of(i * ROWS, ROWS)
        spec = [(a_ref[k1, pl.ds(r, ROWS), :], a_ref[k1, pl.ds(FFT_N2 + r, ROWS), :])
                for k1 in range(FFT_K1)]
        ys = _irfft_sym(spec, FFT_N1, FFT_N1 // 2)
        for s in range(FFT_N1 // 2):
            rows = pl.ds(s * FFT_N2 + r, ROWS)
            u_ref[rows, :] = ys[s] + bias * u_ref[rows, :]
        return carry
    lax.fori_loop(0, FFT_N2 // ROWS, inv_body, 0)

    def out_body(i, carry):
        rows = pl.ds(pl.multiple_of(i * blk, blk), blk)
        x0 = _short_conv(x0_ref[0, rows, :].astype(F32), w0_ref, b0_ref)
        o_ref[0, rows, :] = (x0 * u_ref[rows, :]).astype(o_ref.dtype)
        return carry
    lax.fori_loop(0, SEQ // blk, out_body, 0)


def _hyena_call(hy, conv_w, conv_b, bias, khat, ff, fi, twr, twi):
    bsz = hy.shape[0]
    ng = D_HYENA // CG
    def stream(j):
        return pl.BlockSpec((1, SEQ, CG), lambda g, b: (b, 0, j * ng + g))
    def taps(j):
        return pl.BlockSpec((3, CG), lambda g, b: (0, j * ng + g))
    def offs(j):
        return pl.BlockSpec((1, CG), lambda g, b: (0, j * ng + g))
    def whole(a):
        return pl.BlockSpec(a.shape, lambda g, b: (0,) * a.ndim)
    return pl.pallas_call(
        _hyena_kernel,
        grid=(ng, bsz),
        in_specs=[stream(0), stream(1), stream(2), taps(0), taps(1), taps(2),
                  offs(0), offs(1), offs(2), offs(0),
                  pl.BlockSpec((FFT_K1, 2 * FFT_N2, CG), lambda g, b: (0, 0, g)),
                  whole(ff), whole(fi), whole(twr), whole(twi)],
        out_specs=pl.BlockSpec((1, SEQ, CG), lambda g, b: (b, 0, g)),
        out_shape=jax.ShapeDtypeStruct((bsz, SEQ, D_HYENA), BF16),
        scratch_shapes=[pltpu.VMEM((SEQ, CG), F32),
                        pltpu.VMEM((FFT_K1, 2 * FFT_N2, CG), F32)],
        compiler_params=pltpu.CompilerParams(
            dimension_semantics=("parallel", "parallel"), vmem_limit_bytes=VMEM_LIMIT),
        name="hyena",
    )(hy, hy, hy, conv_w, conv_w, conv_w, conv_b, conv_b, conv_b, bias, khat, ff, fi, twr, twi)


def _out_kernel(x_ref, yh_ref, yg_ref, mod_ref, wo_ref, nm_ref, w1_ref, w2_ref, nf_ref, o_ref):
    d = D_MODEL
    g1 = mod_ref[0, :, 2 * d:3 * d]
    sh2 = mod_ref[0, :, 3 * d:4 * d]
    sc2 = mod_ref[0, :, 4 * d:5 * d]
    g2 = mod_ref[0, :, 5 * d:6 * d]
    mix = _dot(yh_ref[0], wo_ref[0:D_HYENA, :]) + _dot(yg_ref[0], wo_ref[D_HYENA:d, :])
    x1 = x_ref[0] + g1 * mix
    h2 = ((_rms(x1) * nm_ref[...]) * (1.0 + sc2) + sh2).astype(BF16)
    acc = jnp.zeros(x1.shape, F32)
    for j in range(D_FF // d):
        hid = _dot(h2, w1_ref[:, j * d:(j + 1) * d])
        hid = jnp.square(jnp.maximum(hid, 0.0)).astype(BF16)
        acc = acc + _dot(hid, w2_ref[j * d:(j + 1) * d, :])
    x2 = x1 + g2 * acc
    o_ref[0] = _rms(x2) * nf_ref[...]


def _out_call(x, yh, yg, mod, wo, nm, w1, w2, nf, tm):
    bsz, t, _ = x.shape
    def tile(width):
        return pl.BlockSpec((1, tm, width), lambda b, i: (b, i, 0))
    def whole(a):
        return pl.BlockSpec(a.shape, lambda b, i: (0,) * a.ndim)
    return pl.pallas_call(
        _out_kernel,
        grid=(bsz, t // tm),
        in_specs=[tile(D_MODEL), tile(D_HYENA), tile(GLA_DV),
                  pl.BlockSpec((1, 1, 6 * D_MODEL), lambda b, i: (b, 0, 0)),
                  whole(wo), whole(nm), whole(w1), whole(w2), whole(nf)],
        out_specs=tile(D_MODEL),
        out_shape=jax.ShapeDtypeStruct((bsz, t, D_MODEL), F32),
        compiler_params=pltpu.CompilerParams(
            dimension_semantics=("parallel", "parallel"), vmem_limit_bytes=VMEM_LIMIT),
        name="out_mlp",
    )(x, yh, yg, mod, wo, nm, w1, w2, nf)


def _filter_features():
    length = SEQ
    t = jnp.linspace(0.0, 1.0, length, dtype=F32)[:, None]
    w = 2.0 * math.pi * jnp.arange(length, dtype=F32) / length
    f = jnp.linspace(1e-4, FILTER_BANDS - 1, FILTER_BANDS, dtype=F32)
    ang = w[:, None] * f[None, :]
    z = jnp.concatenate([t, jnp.cos(ang), -jnp.sin(ang)], axis=-1)
    def back(a):
        return jnp.concatenate([a[0:1], a[:0:-1]], axis=0)
    z2 = jnp.stack([z, back(z)])
    z2 = jnp.pad(z2, ((0, 0), (0, 0), (0, 128 - FILTER_EMB)))
    t2 = jnp.stack([t, back(t)])
    t2 = jnp.broadcast_to(t2, (2, length, 128))
    min_decay = math.log(DECAY_TARGET) / SLOW_DECAY_PCT
    max_decay = math.log(DECAY_TARGET) / FAST_DECAY_PCT
    deltas = jnp.abs(jnp.linspace(min_decay, max_decay, D_HYENA, dtype=F32))[None, :]
    return z2, t2, deltas


def _dft_tables():
    n2 = jnp.arange(FFT_N2, dtype=jnp.int32)
    k1 = jnp.arange(FFT_K1, dtype=jnp.int32)
    ang_tw = (2.0 * math.pi / FFT_N) * (k1[:, None] * n2[None, :]).astype(F32)
    twr = jnp.broadcast_to(jnp.cos(ang_tw)[:, :, None], (FFT_K1, FFT_N2, 128))
    twi = jnp.broadcast_to(-jnp.sin(ang_tw)[:, :, None], (FFT_K1, FFT_N2, 128))
    ang = (2.0 * math.pi / FFT_N2) * ((n2[:, None] * n2[None, :]) % FFT_N2).astype(F32)
    cr, ci = jnp.cos(ang), -jnp.sin(ang)
    fwd = jnp.block([[cr, -ci], [ci, cr]])
    inv = jnp.block([[cr, ci], [-ci, cr]])
    return fwd, inv, twr, twi


def kernel(x, c, ctx, c_ctx, w_ada, b_ada, norm_mix, norm_mlp, w_in, conv_w, conv_b, filt_w1, filt_b1, filt_w2, filt_b2, filt_w3, filt_b3, filt_freq, filt_wout, hyena_bias, gk_w_fwd, gk_b_fwd, gk_w_bwd, gk_b_bwd, gla_norm, w_out, w_mlp1, w_mlp2, norm_final):
    bsz = x.shape[0]
    l = 0
    cc = jnp.concatenate([c, c_ctx[None, :], jnp.zeros((16 - bsz - 1, D_MODEL), F32)], axis=0)
    mod_all = _mod_call(cc, w_ada[l], b_ada[l][None, :])
    mod = mod_all[:bsz][:, None, :]
    mod_c = jnp.broadcast_to(mod_all[bsz][None, None, :], (bsz, 1, 6 * D_MODEL))

    w_in_p = jnp.pad(w_in[l], ((0, 0), (0, N_IN_PAD - N_IN))).astype(BF16)
    gw = jnp.zeros((N_IN_PAD - GATE_COL, 2 * GLA_DK), F32)
    gw = gw.at[0:GATE_RANK, 0:GLA_DK].set(gk_w_fwd[l])
    gw = gw.at[GATE_RANK:2 * GATE_RANK, GLA_DK:].set(gk_w_bwd[l])
    gb = jnp.concatenate([gk_b_fwd[l], gk_b_bwd[l]])[None, :]
    g_mix = norm_mix[l][None, :]

    hy, q, k, v, og, lgf, lgb = _in_proj_call(x, mod, g_mix, w_in_p, gw, gb, 512)
    _, _, k_c, v_c, _, lgf_c, lgb_c = _in_proj_call(ctx, mod_c, g_mix, w_in_p, gw, gb, ctx.shape[1])

    s_f, s_b = _ctx_state_call(k_c, v_c, lgf_c, lgb_c)
    o_f = _gla_call(q, k, v, lgf, s_f, None, False, 512)
    y_gla = _gla_call(q, k, v, lgb, s_b, (o_f, og, gla_norm[l][None, :]), True, 512)

    z2, t2, deltas = _filter_features()
    f2_fwd, f2_inv, twr, twi = _dft_tables()
    w1p = jnp.pad(filt_w1[l], ((0, 128 - FILTER_EMB), (0, 0)))
    hdn = _filter_mlp_call(z2, w1p, filt_b1[l][None, :], filt_w2[l], filt_b2[l][None, :],
                           filt_w3[l], filt_b3[l][None, :], filt_freq[l][None, :])
    khat = _filter_call(hdn, filt_wout[l], t2, deltas, f2_fwd, twr, twi)
    y_hy = _hyena_call(hy, conv_w[l], conv_b[l][None, :], hyena_bias[l][None, :], khat,
                       f2_fwd.astype(BF16), f2_inv.astype(BF16), twr, twi)

    return _out_call(x, y_hy, y_gla, mod, w_out[l].astype(BF16), norm_mlp[l][None, :],
                     w_mlp1[l].astype(BF16), w_mlp2[l].astype(BF16), norm_final[None, :], 512)
```

```python
import functools
import math

import jax
import jax.numpy as jnp
from jax import lax
from jax.experimental import pallas as pl
from jax.experimental.pallas import tpu as pltpu

D_MODEL = 1024
SEQ = 4096
GRID_W = 64
D_HYENA = 512
GLA_HEADS = 4
GLA_DK = 256
GLA_DV = 512
GLA_HEAD_DK = 64
GLA_HEAD_DV = 128
GATE_RANK = 16
GATE_NORMALIZER = 16.0
GLA_CHUNK = 64
FILTER_EMB = 33
FILTER_BANDS = 16
FILTER_HIDDEN = 64
FAST_DECAY_PCT = 0.3
SLOW_DECAY_PCT = 1.5
DECAY_TARGET = 1e-2
D_FF = 4 * D_MODEL
EPS = 1e-6
N_IN = 3104
N_IN_PAD = 3200
GATE_COL = 3072

FFT_N = 2 * SEQ
FFT_N1 = 32
FFT_N2 = 256
FFT_K1 = FFT_N1 // 2 + 1
CG = 256
CGF = 128
ROWS = 8

F32 = jnp.float32
BF16 = jnp.bfloat16
HI = lax.Precision.HIGHEST
VMEM_LIMIT = 56 * 1024 * 1024


def _dot(a, b, precision=None):
    return jnp.dot(a, b, preferred_element_type=F32, precision=precision)


def _add(a, b):
    if a is None:
        return b
    if b is None:
        return a
    return a + b


def _sub(a, b):
    if b is None:
        return a
    if a is None:
        return -b
    return a - b


def _scale(a, c):
    if a is None or c == 0.0:
        return None
    if c == 1.0:
        return a
    if c == -1.0:
        return -a
    return a * c


def _cadd(x, y):
    return (_add(x[0], y[0]), _add(x[1], y[1]))


def _csub(x, y):
    return (_sub(x[0], y[0]), _sub(x[1], y[1]))


def _conj(x):
    return (x[0], None if x[1] is None else -x[1])


def _snap(v):
    for t in (0.0, 1.0, -1.0):
        if abs(v - t) < 1e-12:
            return t
    return v


def _cmul_const(x, ang):
    c, s = _snap(math.cos(ang)), _snap(math.sin(ang))
    re = _sub(_scale(x[0], c), _scale(x[1], s))
    im = _add(_scale(x[0], s), _scale(x[1], c))
    return (re, im)


def _rfft_sym(x):
    n = len(x)
    if n == 1:
        return [(x[0], None)]
    if n == 2:
        return [(_add(x[0], x[1]), None), (_sub(x[0], x[1]), None)]
    ev = _rfft_sym(x[0::2])
    od = _rfft_sym(x[1::2])
    out = [None] * (n // 2 + 1)
    for k in range(n // 4 + 1):
        t = _cmul_const(od[k], -2.0 * math.pi * k / n)
        out[k] = _cadd(ev[k], t)
        if n // 2 - k != k:
            out[n // 2 - k] = _conj(_csub(ev[k], t))
    out[0] = (out[0][0], None)
    out[n // 2] = (out[n // 2][0], None)
    return out


def _irfft_sym(b, n, keep):
    if keep <= 0:
        return [None] * n
    if n == 1:
        return [b[0][0]]
    if n == 2:
        y1 = _sub(b[0][0], b[1][0]) if keep > 1 else None
        return [_add(b[0][0], b[1][0]), y1]
    b = list(b)
    b[0] = (b[0][0], None)
    b[n // 2] = (b[n // 2][0], None)
    ev, od = [], []
    for k in range(n // 4 + 1):
        cb = _conj(b[n // 2 - k])
        ev.append(_cadd(b[k], cb))
        od.append(_cmul_const(_csub(b[k], cb), 2.0 * math.pi * k / n))
    y = [None] * n
    y[0::2] = _irfft_sym(ev, n // 2, (keep + 1) // 2)
    y[1::2] = _irfft_sym(od, n // 2, keep // 2)
    return y


def _slow_forward(load_slab, n_slabs, a_ref, width):
    def body(i, carry):
        r = pl.multiple_of(i * ROWS, ROWS)
        xs = [load_slab(s, r) for s in range(n_slabs)] + [None] * (FFT_N1 - n_slabs)
        spec = _rfft_sym(xs)
        zero = jnp.zeros((ROWS, width), F32)
        for k1 in range(FFT_K1):
            re, im = spec[k1]
            a_ref[k1, pl.ds(r, ROWS), :] = zero if re is None else re
            a_ref[k1, pl.ds(FFT_N2 + r, ROWS), :] = zero if im is None else im
        return carry
    lax.fori_loop(0, FFT_N2 // ROWS, body, 0)


def _tile_lanes(t, width):
    reps = width // t.shape[-1]
    return t if reps == 1 else jnp.concatenate([t] * reps, axis=-1)


def _mod_kernel(c_ref, w_ref, b_ref, o_ref):
    c = c_ref[...]
    s = c * jax.nn.sigmoid(c)
    o_ref[...] = _dot(s, w_ref[...], HI) + b_ref[...]


def _mod_call(cc, w, b):
    n = w.shape[1]
    bn = 1024
    return pl.pallas_call(
        _mod_kernel,
        grid=(n // bn,),
        in_specs=[pl.BlockSpec(cc.shape, lambda j: (0, 0)),
                  pl.BlockSpec((D_MODEL, bn), lambda j: (0, j)),
                  pl.BlockSpec((1, bn), lambda j: (0, j))],
        out_specs=pl.BlockSpec((cc.shape[0], bn), lambda j: (0, j)),
        out_shape=jax.ShapeDtypeStruct((cc.shape[0], n), F32),
        compiler_params=pltpu.CompilerParams(vmem_limit_bytes=VMEM_LIMIT),
        name="mod",
    )(cc, w, b)


def _rms(x):
    return x * lax.rsqrt(jnp.mean(x * x, axis=-1, keepdims=True) + EPS)


def _short_conv(x, w, b):
    n = x.shape[0]
    pos = lax.broadcasted_iota(jnp.int32, x.shape, 0) & (GRID_W - 1)
    prev = jnp.where(pos == 0, 0.0, pltpu.roll(x, 1, 0))
    nxt = jnp.where(pos == GRID_W - 1, 0.0, pltpu.roll(x, n - 1, 0))
    return b + w[0:1, :] * prev + w[1:2, :] * x + w[2:3, :] * nxt


def _in_proj_kernel(*refs, latent):
    if latent:
        (x_ref, mod_ref, g_ref, w_ref, gw_ref, gb_ref, cw_ref, cb_ref,
         x0_ref, u_ref, q_ref, k_ref, v_ref, og_ref, lgf_ref, lgb_ref) = refs
    else:
        x_ref, mod_ref, g_ref, w_ref, gw_ref, gb_ref, k_ref, v_ref, lgf_ref, lgb_ref = refs
    x = x_ref[0]
    sh = mod_ref[0, :, 0:D_MODEL]
    sc = mod_ref[0, :, D_MODEL:2 * D_MODEL]
    h = ((_rms(x) * g_ref[...]) * (1.0 + sc) + sh).astype(BF16)
    c0 = 3 * D_HYENA
    c1 = c0 + GLA_DK
    c2 = c1 + GLA_DK
    c3 = c2 + GLA_DV
    if latent:
        dh = D_HYENA
        conv = [_short_conv(_dot(h, w_ref[:, j * dh:(j + 1) * dh]),
                            cw_ref[:, j * dh:(j + 1) * dh], cb_ref[:, j * dh:(j + 1) * dh])
                for j in range(3)]
        x0_ref[0] = conv[0].astype(BF16)
        u_ref[0] = (conv[1] * conv[2]).astype(BF16)
        q = _dot(h, w_ref[:, c0:c1]) * (GLA_HEAD_DK ** -0.5)
        q_ref[0] = q.astype(BF16)
        og_ref[0] = _dot(h, w_ref[:, c3:c3 + GLA_DV]).astype(BF16)
    k_ref[0] = _dot(h, w_ref[:, c1:c2]).astype(BF16)
    v_ref[0] = _dot(h, w_ref[:, c2:c3]).astype(BF16)
    gr = _dot(h, w_ref[:, GATE_COL:N_IN_PAD]).astype(BF16)
    z = _dot(gr, gw_ref[...]) + gb_ref[...]
    lg = jax.nn.log_sigmoid(z) * (1.0 / GATE_NORMALIZER)
    lgf_ref[0] = lg[:, 0:GLA_DK]
    lgb_ref[0] = lg[:, GLA_DK:2 * GLA_DK]


def _in_proj_call(x, mod, g, w, gw, gb, conv, tm):
    bsz, t, _ = x.shape
    latent = conv is not None
    def tile(width):
        return pl.BlockSpec((1, tm, width), lambda b, i: (b, i, 0))
    def whole(a):
        return pl.BlockSpec(a.shape, lambda b, i: (0,) * a.ndim)
    if latent:
        widths = (D_HYENA, D_HYENA, GLA_DK, GLA_DK, GLA_DV, GLA_DV, GLA_DK, GLA_DK)
        dtypes = (BF16,) * 6 + (F32, F32)
    else:
        widths = (GLA_DK, GLA_DV, GLA_DK, GLA_DK)
        dtypes = (BF16, BF16, F32, F32)
    args = [x, mod, g, w, gw, gb] + (list(conv) if latent else [])
    return pl.pallas_call(
        functools.partial(_in_proj_kernel, latent=latent),
        grid=(bsz, t // tm),
        in_specs=[tile(D_MODEL),
                  pl.BlockSpec((1, 1, 6 * D_MODEL), lambda b, i: (b, 0, 0))]
                 + [whole(a) for a in args[2:]],
        out_specs=[tile(wd) for wd in widths],
        out_shape=[jax.ShapeDtypeStruct((bsz, t, wd), dt) for wd, dt in zip(widths, dtypes)],
        compiler_params=pltpu.CompilerParams(
            dimension_semantics=("parallel", "parallel"), vmem_limit_bytes=VMEM_LIMIT),
        name="in_proj" if latent else "in_proj_ctx",
    )(*args)


def _ctx_state_kernel(k_ref, v_ref, lgf_ref, lgb_ref, sf_ref, sb_ref):
    t = k_ref.shape[1]
    row = lax.broadcasted_iota(jnp.int32, (t, t), 0)
    col = lax.broadcasted_iota(jnp.int32, (t, t), 1)
    k = k_ref[0].astype(F32)
    v = v_ref[0]
    for lg_ref, tri, o_ref in ((lgf_ref, col > row, sf_ref), (lgb_ref, col < row, sb_ref)):
        rem = _dot(tri.astype(F32), lg_ref[0], HI)
        kend = (k * jnp.exp(rem)).astype(BF16)
        ds = lax.dot_general(kend, v, (((0,), (0,)), ((), ())), preferred_element_type=F32)
        for h in range(GLA_HEADS):
            o_ref[0, h] = ds[h * GLA_HEAD_DK:(h + 1) * GLA_HEAD_DK,
                             h * GLA_HEAD_DV:(h + 1) * GLA_HEAD_DV]


def _ctx_state_call(k, v, lgf, lgb):
    bsz, t, _ = k.shape
    def tile(width):
        return pl.BlockSpec((1, t, width), lambda b: (b, 0, 0))
    st = pl.BlockSpec((1, GLA_HEADS, GLA_HEAD_DK, GLA_HEAD_DV), lambda b: (b, 0, 0, 0))
    return pl.pallas_call(
        _ctx_state_kernel,
        grid=(bsz,),
        in_specs=[tile(GLA_DK), tile(GLA_DV), tile(GLA_DK), tile(GLA_DK)],
        out_specs=[st, st],
        out_shape=[jax.ShapeDtypeStruct((bsz, GLA_HEADS, GLA_HEAD_DK, GLA_HEAD_DV), F32)] * 2,
        compiler_params=pltpu.CompilerParams(vmem_limit_bytes=VMEM_LIMIT),
        name="gla_ctx",
    )(k, v, lgf, lgb)


def _prefix_rows(x, reverse):
    n = x.shape[0]
    pos = lax.broadcasted_iota(jnp.int32, x.shape, 0) & (GLA_CHUNK - 1)
    k = 1
    while k < GLA_CHUNK:
        if reverse:
            x = x + jnp.where(pos < GLA_CHUNK - k, pltpu.roll(x, n - k, 0), 0.0)
        else:
            x = x + jnp.where(pos >= k, pltpu.roll(x, k, 0), 0.0)
        k *= 2
    return x


def _pair_diag(a, b):
    z = jnp.zeros_like(a)
    return jnp.concatenate([jnp.concatenate([a, z], axis=1),
                            jnp.concatenate([z, b], axis=1)], axis=0)


def _gla_chunk(q, k, v, b, s, reverse):
    c, hd, hv = GLA_CHUNK, GLA_HEAD_DK, GLA_HEAD_DV
    total = b[0:1] if reverse else b[c - 1:c]
    qin = (q * jnp.exp(b)).astype(BF16)
    kin = k * jnp.exp(-b)
    kend = k * jnp.exp(total - b)
    dec = jnp.exp(total)
    hi = dec.astype(BF16).astype(F32)
    mid = (dec - hi).astype(BF16).astype(F32)
    lo = dec - hi - mid
    r8 = lax.broadcasted_iota(jnp.int32, (8, GLA_DK), 0)
    extra = jnp.where(r8 == 0, hi, jnp.where(r8 == 1, mid, jnp.where(r8 == 2, lo, 0.0)))
    a = jnp.concatenate([kend, extra, jnp.zeros((c - 8, GLA_DK), F32)], axis=0)
    at = a.T.astype(BF16)
    r64 = lax.broadcasted_iota(jnp.int32, (c, 2 * hv), 0)
    l64 = lax.broadcasted_iota(jnp.int32, (c, 2 * hv), 1)
    ones_blk = jnp.where((r64 < 3) & (l64 >= hv), 1.0, 0.0).astype(BF16)
    zv = jnp.zeros((c, hv), BF16)
    key = lax.broadcasted_iota(jnp.int32, (c, 2 * hd), 1)
    qrow = lax.broadcasted_iota(jnp.int32, (c, 2 * hd), 0)
    att_mask = ((key & (hd - 1)) >= qrow) if reverse else ((key & (hd - 1)) <= qrow)
    outs, s_new = [], list(s)
    for p in range(GLA_HEADS // 2):
        lanes = slice(p * 2 * hd, (p + 1) * 2 * hd)
        q2, k2 = qin[:, lanes], kin[:, lanes]
        kbd = jnp.concatenate([jnp.where(key < hd, k2, 0.0), jnp.where(key >= hd, k2, 0.0)],
                              axis=0).astype(BF16)
        att = lax.dot_general(q2, kbd, (((1,), (1,)), ((), ())), preferred_element_type=F32)
        att = jnp.where(att_mask, att, 0.0).astype(BF16)
        h0, h1 = 2 * p, 2 * p + 1
        v0, v1 = v[:, h0 * hv:(h0 + 1) * hv], v[:, h1 * hv:(h1 + 1) * hv]
        rhs = jnp.concatenate([_pair_diag(v0, v1),
                               _pair_diag(s[h0].astype(BF16), s[h1].astype(BF16))], axis=0)
        outs.append(_dot(jnp.concatenate([att, q2], axis=1), rhs))
        for h, vh in ((h0, v0), (h1, v1)):
            inc = _dot(at[h * hd:(h + 1) * hd, :],
                       jnp.concatenate([jnp.concatenate([vh, zv], axis=1), ones_blk], axis=0))
            s_new[h] = inc[:, hv:] * s[h] + inc[:, :hv]
    return jnp.concatenate(outs, axis=1), s_new


def _gla_kernel(*refs, reverse):
    if reverse:
        q_ref, k_ref, v_ref, lg_ref, s0_ref, of_ref, og_ref, gn_ref, o_ref, s_scr, b_scr = refs
    else:
        q_ref, k_ref, v_ref, lg_ref, s0_ref, o_ref, s_scr, b_scr = refs
    c = GLA_CHUNK

    @pl.when(pl.program_id(1) == 0)
    def _():
        s_scr[...] = s0_ref[0]

    b_scr[...] = _prefix_rows(lg_ref[0], reverse)
    s = [s_scr[h] for h in range(GLA_HEADS)]
    n_chunks = q_ref.shape[1] // c
    order = range(n_chunks - 1, -1, -1) if reverse else range(n_chunks)
    for ci in order:
        rows = slice(ci * c, (ci + 1) * c)
        o, s = _gla_chunk(q_ref[0, rows, :].astype(F32), k_ref[0, rows, :].astype(F32),
                          v_ref[0, rows, :], b_scr[rows, :], s, reverse)
        if reverse:
            o = o + of_ref[0, rows, :]
            og = og_ref[0, rows, :].astype(F32)
            parts = []
            for h in range(GLA_HEADS):
                oh = o[:, h * GLA_HEAD_DV:(h + 1) * GLA_HEAD_DV]
                parts.append(_rms(oh) * gn_ref[...])
            y = jnp.concatenate(parts, axis=1) * (og * jax.nn.sigmoid(og))
            o_ref[0, rows, :] = y.astype(o_ref.dtype)
        else:
            o_ref[0, rows, :] = o
    for h in range(GLA_HEADS):
        s_scr[h] = s[h]


def _gla_call(q, k, v, lg, s0, extra, reverse, tt):
    bsz, t, _ = q.shape
    nt = t // tt
    if reverse:
        def tile(width):
            return pl.BlockSpec((1, tt, width), lambda b, j: (b, nt - 1 - j, 0))
    else:
        def tile(width):
            return pl.BlockSpec((1, tt, width), lambda b, j: (b, j, 0))
    in_specs = [tile(GLA_DK), tile(GLA_DK), tile(GLA_DV), tile(GLA_DK),
                pl.BlockSpec((1, GLA_HEADS, GLA_HEAD_DK, GLA_HEAD_DV), lambda b, j: (b, 0, 0, 0))]
    args = [q, k, v, lg, s0]
    if reverse:
        o_f, og, gn = extra
        in_specs += [tile(GLA_DV), tile(GLA_DV), pl.BlockSpec(gn.shape, lambda b, j: (0, 0))]
        args += [o_f, og, gn]
    return pl.pallas_call(
        functools.partial(_gla_kernel, reverse=reverse),
        grid=(bsz, nt),
        in_specs=in_specs,
        out_specs=tile(GLA_DV),
        out_shape=jax.ShapeDtypeStruct((bsz, t, GLA_DV), BF16 if reverse else F32),
        scratch_shapes=[pltpu.VMEM((GLA_HEADS, GLA_HEAD_DK, GLA_HEAD_DV), F32),
                        pltpu.VMEM((tt, GLA_DK), F32)],
        compiler_params=pltpu.CompilerParams(
            dimension_semantics=("parallel", "arbitrary"), vmem_limit_bytes=VMEM_LIMIT),
        name="gla_bwd" if reverse else "gla_fwd",
    )(*args)


def _filter_mlp_kernel(z_ref, w1_ref, b1_ref, w2_ref, b2_ref, w3_ref, b3_ref, fr_ref, o_ref):
    fr = fr_ref[...]
    for d in range(2):
        h = jnp.sin(fr * (_dot(z_ref[d], w1_ref[...], HI) + b1_ref[...]))
        h = jnp.sin(fr * (_dot(h, w2_ref[...], HI) + b2_ref[...]))
        h = jnp.sin(fr * (_dot(h, w3_ref[...], HI) + b3_ref[...]))
        o_ref[d] = h


def _filter_mlp_call(z2, w1p, b1, w2, b2, w3, b3, fr):
    return pl.pallas_call(
        _filter_mlp_kernel,
        out_shape=jax.ShapeDtypeStruct((2, SEQ, FILTER_HIDDEN), F32),
        compiler_params=pltpu.CompilerParams(vmem_limit_bytes=VMEM_LIMIT),
        name="filter_mlp",
    )(z2, w1p, b1, w2, b2, w3, b3, fr)


def _filter_kernel(hdn_ref, wf_ref, wb_ref, t_ref, dl_ref, f2_ref, twr_ref, twi_ref,
                   khat_ref, kern_ref, a_ref):
    dl = dl_ref[...]
    blk = 512

    def taps_body(i, ss):
        r0 = pl.multiple_of(i * blk, blk)
        rows = pl.ds(r0, blk)
        hf = _dot(hdn_ref[0, rows, :], wf_ref[...], HI) * jnp.exp(-t_ref[0, rows, :] * dl)
        hb = _dot(hdn_ref[1, rows, :], wb_ref[...], HI) * jnp.exp(-t_ref[1, rows, :] * dl)
        lag = r0 + lax.broadcasted_iota(jnp.int32, hb.shape, 0)
        hb = jnp.where(lag == 0, 0.0, hb)
        kern_ref[rows, :] = hf
        kern_ref[pl.ds(SEQ + r0, blk), :] = hb
        return (ss + jnp.sum(hf * hf, axis=0, keepdims=True)
                + jnp.sum(hb * hb, axis=0, keepdims=True))
    ss = lax.fori_loop(0, SEQ // blk, taps_body, jnp.zeros((1, CGF), F32))
    scale = lax.rsqrt(ss + EPS) * (1.0 / FFT_N)

    def load_slab(s, r):
        return kern_ref[pl.ds(s * FFT_N2 + r, ROWS), :]
    _slow_forward(load_slab, FFT_N1, a_ref, CGF)

    def body(k1, carry):
        ar = a_ref[k1, 0:FFT_N2, :]
        ai = a_ref[k1, FFT_N2:2 * FFT_N2, :]
        twr = twr_ref[k1]
        twi = twi_ref[k1]
        a = jnp.concatenate([ar * twr - ai * twi, ar * twi + ai * twr], axis=0)
        khat_ref[k1] = _dot(f2_ref[...], a, HI) * scale
        return carry
    lax.fori_loop(0, FFT_K1, body, 0)


def _filter_call(hdn, wout, t2, deltas, f2, twr, twi):
    ng = D_HYENA // CGF
    def whole(a):
        return pl.BlockSpec(a.shape, lambda g: (0,) * a.ndim)
    return pl.pallas_call(
        _filter_kernel,
        grid=(ng,),
        in_specs=[whole(hdn),
                  pl.BlockSpec((FILTER_HIDDEN, CGF), lambda g: (0, g)),
                  pl.BlockSpec((FILTER_HIDDEN, CGF), lambda g: (0, ng + g)),
                  whole(t2),
                  pl.BlockSpec((1, CGF), lambda g: (0, g)),
                  whole(f2), whole(twr), whole(twi)],
        out_specs=pl.BlockSpec((FFT_K1, 2 * FFT_N2, CGF), lambda g: (0, 0, g)),
        out_shape=jax.ShapeDtypeStruct((FFT_K1, 2 * FFT_N2, D_HYENA), F32),
        scratch_shapes=[pltpu.VMEM((FFT_N, CGF), F32),
                        pltpu.VMEM((FFT_K1, 2 * FFT_N2, CGF), F32)],
        compiler_params=pltpu.CompilerParams(vmem_limit_bytes=VMEM_LIMIT),
        name="filter_spec",
    )(hdn, wout, wout, t2, deltas, f2, twr, twi)


def _freq_step(k1, a_ref, khat_ref, ff_ref, fi_ref, twr_ref, twi_ref):
    ar = a_ref[k1, 0:FFT_N2, :]
    ai = a_ref[k1, FFT_N2:2 * FFT_N2, :]
    twr = _tile_lanes(twr_ref[k1], CG)
    twi = _tile_lanes(twi_ref[k1], CG)
    a = jnp.concatenate([ar * twr - ai * twi, ar * twi + ai * twr], axis=0).astype(BF16)
    x = _dot(ff_ref[...], a)
    xr, xi = x[0:FFT_N2], x[FFT_N2:]
    kr = khat_ref[k1, 0:FFT_N2, :]
    ki = khat_ref[k1, FFT_N2:2 * FFT_N2, :]
    y = jnp.concatenate([xr * kr - xi * ki, xr * ki + xi * kr], axis=0).astype(BF16)
    bq = _dot(fi_ref[...], y)
    br, bi = bq[0:FFT_N2], bq[FFT_N2:]
    a_ref[k1, 0:FFT_N2, :] = br * twr + bi * twi
    a_ref[k1, FFT_N2:2 * FFT_N2, :] = bi * twr - br * twi


def _hyena_kernel(x0_ref, uin_ref, bias_ref, khat_ref, ff_ref, fi_ref, twr_ref, twi_ref,
                  o_ref, u_ref, a_ref):
    blk = FFT_N2

    def load_body(i, carry):
        rows = pl.ds(pl.multiple_of(i * blk, blk), blk)
        u_ref[rows, :] = uin_ref[0, rows, :].astype(F32)
        return carry
    lax.fori_loop(0, SEQ // blk, load_body, 0)

    def load_slab(s, r):
        return u_ref[pl.ds(s * FFT_N2 + r, ROWS), :]
    _slow_forward(load_slab, FFT_N1 // 2, a_ref, CG)

    def freq_body(i, carry):
        for k1 in (2 * i, 2 * i + 1):
            _freq_step(k1, a_ref, khat_ref, ff_ref, fi_ref, twr_ref, twi_ref)
        return carry
    lax.fori_loop(0, FFT_K1 // 2, freq_body, 0)
    _freq_step(FFT_K1 - 1, a_ref, khat_ref, ff_ref, fi_ref, twr_ref, twi_ref)

    bias = bias_ref[...]

    def inv_body(i, carry):
        r = pl.multiple_of(i * ROWS, ROWS)
        spec = [(a_ref[k1, pl.ds(r, ROWS), :], a_ref[k1, pl.ds(FFT_N2 + r, ROWS), :])
                for k1 in range(FFT_K1)]
        ys = _irfft_sym(spec, FFT_N1, FFT_N1 // 2)
        for s in range(FFT_N1 // 2):
            rows = pl.ds(s * FFT_N2 + r, ROWS)
            u_ref[rows, :] = ys[s] + bias * u_ref[rows, :]
        return carry
    lax.fori_loop(0, FFT_N2 // ROWS, inv_body, 0)

    def out_body(i, carry):
        rows = pl.ds(pl.multiple_of(i * blk, blk), blk)
        o_ref[0, rows, :] = (x0_ref[0, rows, :].astype(F32) * u_ref[rows, :]).astype(o_ref.dtype)
        return carry
    lax.fori_loop(0, SEQ // blk, out_body, 0)


def _hyena_call(x0c, u, bias, khat, ff, fi, twr, twi):
    bsz = u.shape[0]
    ng = D_HYENA // CG
    stream = pl.BlockSpec((1, SEQ, CG), lambda g, b: (b, 0, g))
    def whole(a):
        return pl.BlockSpec(a.shape, lambda g, b: (0,) * a.ndim)
    return pl.pallas_call(
        _hyena_kernel,
        grid=(ng, bsz),
        in_specs=[stream, stream,
                  pl.BlockSpec((1, CG), lambda g, b: (0, g)),
                  pl.BlockSpec((FFT_K1, 2 * FFT_N2, CG), lambda g, b: (0, 0, g)),
                  whole(ff), whole(fi), whole(twr), whole(twi)],
        out_specs=stream,
        out_shape=jax.ShapeDtypeStruct((bsz, SEQ, D_HYENA), BF16),
        scratch_shapes=[pltpu.VMEM((SEQ, CG), F32),
                        pltpu.VMEM((FFT_K1, 2 * FFT_N2, CG), F32)],
        compiler_params=pltpu.CompilerParams(
            dimension_semantics=("parallel", "parallel"), vmem_limit_bytes=VMEM_LIMIT),
        name="hyena",
    )(x0c, u, bias, khat, ff, fi, twr, twi)


def _out_kernel(x_ref, yh_ref, yg_ref, mod_ref, wo_ref, nm_ref, w1_ref, w2_ref, nf_ref, o_ref):
    d = D_MODEL
    g1 = mod_ref[0, :, 2 * d:3 * d]
    sh2 = mod_ref[0, :, 3 * d:4 * d]
    sc2 = mod_ref[0, :, 4 * d:5 * d]
    g2 = mod_ref[0, :, 5 * d:6 * d]
    mix = _dot(yh_ref[0], wo_ref[0:D_HYENA, :]) + _dot(yg_ref[0], wo_ref[D_HYENA:d, :])
    x1 = x_ref[0] + g1 * mix
    h2 = ((_rms(x1) * nm_ref[...]) * (1.0 + sc2) + sh2).astype(BF16)
    acc = jnp.zeros(x1.shape, F32)
    for j in range(D_FF // d):
        hid = _dot(h2, w1_ref[:, j * d:(j + 1) * d])
        hid = jnp.square(jnp.maximum(hid, 0.0)).astype(BF16)
        acc = acc + _dot(hid, w2_ref[j * d:(j + 1) * d, :])
    x2 = x1 + g2 * acc
    o_ref[0] = _rms(x2) * nf_ref[...]


def _out_call(x, yh, yg, mod, wo, nm, w1, w2, nf, tm):
    bsz, t, _ = x.shape
    def tile(width):
        return pl.BlockSpec((1, tm, width), lambda b, i: (b, i, 0))
    def whole(a):
        return pl.BlockSpec(a.shape, lambda b, i: (0,) * a.ndim)
    return pl.pallas_call(
        _out_kernel,
        grid=(bsz, t // tm),
        in_specs=[tile(D_MODEL), tile(D_HYENA), tile(GLA_DV),
                  pl.BlockSpec((1, 1, 6 * D_MODEL), lambda b, i: (b, 0, 0)),
                  whole(wo), whole(nm), whole(w1), whole(w2), whole(nf)],
        out_specs=tile(D_MODEL),
        out_shape=jax.ShapeDtypeStruct((bsz, t, D_MODEL), F32),
        compiler_params=pltpu.CompilerParams(
            dimension_semantics=("parallel", "parallel"), vmem_limit_bytes=VMEM_LIMIT),
        name="out_mlp",
    )(x, yh, yg, mod, wo, nm, w1, w2, nf)


def _filter_features():
    length = SEQ
    t = jnp.linspace(0.0, 1.0, length, dtype=F32)[:, None]
    w = 2.0 * math.pi * jnp.arange(length, dtype=F32) / length
    f = jnp.linspace(1e-4, FILTER_BANDS - 1, FILTER_BANDS, dtype=F32)
    ang = w[:, None] * f[None, :]
    z = jnp.concatenate([t, jnp.cos(ang), -jnp.sin(ang)], axis=-1)
    def back(a):
        return jnp.concatenate([a[0:1], a[:0:-1]], axis=0)
    z2 = jnp.stack([z, back(z)])
    z2 = jnp.pad(z2, ((0, 0), (0, 0), (0, 128 - FILTER_EMB)))
    t2 = jnp.stack([t, back(t)])
    t2 = jnp.broadcast_to(t2, (2, length, 128))
    min_decay = math.log(DECAY_TARGET) / SLOW_DECAY_PCT
    max_decay = math.log(DECAY_TARGET) / FAST_DECAY_PCT
    deltas = jnp.abs(jnp.linspace(min_decay, max_decay, D_HYENA, dtype=F32))[None, :]
    return z2, t2, deltas


def _dft_tables():
    n2 = jnp.arange(FFT_N2, dtype=jnp.int32)
    k1 = jnp.arange(FFT_K1, dtype=jnp.int32)
    ang_tw = (2.0 * math.pi / FFT_N) * (k1[:, None] * n2[None, :]).astype(F32)
    twr = jnp.broadcast_to(jnp.cos(ang_tw)[:, :, None], (FFT_K1, FFT_N2, 128))
    twi = jnp.broadcast_to(-jnp.sin(ang_tw)[:, :, None], (FFT_K1, FFT_N2, 128))
    ang = (2.0 * math.pi / FFT_N2) * ((n2[:, None] * n2[None, :]) % FFT_N2).astype(F32)
    cr, ci = jnp.cos(ang), -jnp.sin(ang)
    fwd = jnp.block([[cr, -ci], [ci, cr]])
    inv = jnp.block([[cr, ci], [-ci, cr]])
    return fwd, inv, twr, twi


def kernel(x, c, ctx, c_ctx, w_ada, b_ada, norm_mix, norm_mlp, w_in, conv_w, conv_b, filt_w1, filt_b1, filt_w2, filt_b2, filt_w3, filt_b3, filt_freq, filt_wout, hyena_bias, gk_w_fwd, gk_b_fwd, gk_w_bwd, gk_b_bwd, gla_norm, w_out, w_mlp1, w_mlp2, norm_final):
    bsz = x.shape[0]
    l = 0
    cc = jnp.concatenate([c, c_ctx[None, :], jnp.zeros((16 - bsz - 1, D_MODEL), F32)], axis=0)
    mod_all = _mod_call(cc, w_ada[l], b_ada[l][None, :])
    mod = mod_all[:bsz][:, None, :]
    mod_c = jnp.broadcast_to(mod_all[bsz][None, None, :], (bsz, 1, 6 * D_MODEL))

    w_in_p = jnp.pad(w_in[l], ((0, 0), (0, N_IN_PAD - N_IN))).astype(BF16)
    gw = jnp.zeros((N_IN_PAD - GATE_COL, 2 * GLA_DK), F32)
    gw = gw.at[0:GATE_RANK, 0:GLA_DK].set(gk_w_fwd[l])
    gw = gw.at[GATE_RANK:2 * GATE_RANK, GLA_DK:].set(gk_w_bwd[l])
    gb = jnp.concatenate([gk_b_fwd[l], gk_b_bwd[l]])[None, :]
    gw = gw.astype(BF16)
    g_mix = norm_mix[l][None, :]

    conv = (conv_w[l], conv_b[l][None, :])
    x0c, u, q, k, v, og, lgf, lgb = _in_proj_call(x, mod, g_mix, w_in_p, gw, gb, conv, 512)
    k_c, v_c, lgf_c, lgb_c = _in_proj_call(ctx, mod_c, g_mix, w_in_p, gw, gb, None, ctx.shape[1])

    s_f, s_b = _ctx_state_call(k_c, v_c, lgf_c, lgb_c)
    o_f = _gla_call(q, k, v, lgf, s_f, None, False, 512)
    y_gla = _gla_call(q, k, v, lgb, s_b, (o_f, og, gla_norm[l][None, :]), True, 512)

    z2, t2, deltas = _filter_features()
    f2_fwd, f2_inv, twr, twi = _dft_tables()
    w1p = jnp.pad(filt_w1[l], ((0, 128 - FILTER_EMB), (0, 0)))
    hdn = _filter_mlp_call(z2, w1p, filt_b1[l][None, :], filt_w2[l], filt_b2[l][None, :],
                           filt_w3[l], filt_b3[l][None, :], filt_freq[l][None, :])
    khat = _filter_call(hdn, filt_wout[l], t2, deltas, f2_fwd, twr, twi)
    y_hy = _hyena_call(x0c, u, hyena_bias[l][None, :], khat,
                       f2_fwd.astype(BF16), f2_inv.astype(BF16), twr, twi)

    return _out_call(x, y_hy, y_gla, mod, w_out[l].astype(BF16), norm_mlp[l][None, :],
                     w_mlp1[l].astype(BF16), w_mlp2[l].astype(BF16), norm_final[None, :], 512)
```

```python
import functools
import math

import jax
import jax.numpy as jnp
from jax import lax
from jax.experimental import pallas as pl
from jax.experimental.pallas import tpu as pltpu

D_MODEL = 1024
SEQ = 4096
GRID_W = 64
D_HYENA = 512
GLA_HEADS = 4
GLA_DK = 256
GLA_DV = 512
GLA_HEAD_DK = 64
GLA_HEAD_DV = 128
GATE_RANK = 16
GATE_NORMALIZER = 16.0
GLA_CHUNK = 64
FILTER_EMB = 33
FILTER_BANDS = 16
FILTER_HIDDEN = 64
FAST_DECAY_PCT = 0.3
SLOW_DECAY_PCT = 1.5
DECAY_TARGET = 1e-2
D_FF = 4 * D_MODEL
EPS = 1e-6
N_IN = 3104
N_IN_PAD = 3200
GATE_COL = 3072

FFT_N = 2 * SEQ
FFT_N1 = 32
FFT_N2 = 256
FFT_K1 = FFT_N1 // 2 + 1
CG = 256
ROWS = 8

F32 = jnp.float32
BF16 = jnp.bfloat16
HI = lax.Precision.HIGHEST
VMEM_LIMIT = 56 * 1024 * 1024


def _dot(a, b, precision=None):
    return jnp.dot(a, b, preferred_element_type=F32, precision=precision)


def _add(a, b):
    if a is None:
        return b
    if b is None:
        return a
    return a + b


def _sub(a, b):
    if b is None:
        return a
    if a is None:
        return -b
    return a - b


def _scale(a, c):
    if a is None or c == 0.0:
        return None
    if c == 1.0:
        return a
    if c == -1.0:
        return -a
    return a * c


def _cadd(x, y):
    return (_add(x[0], y[0]), _add(x[1], y[1]))


def _csub(x, y):
    return (_sub(x[0], y[0]), _sub(x[1], y[1]))


def _conj(x):
    return (x[0], None if x[1] is None else -x[1])


def _snap(v):
    for t in (0.0, 1.0, -1.0):
        if abs(v - t) < 1e-12:
            return t
    return v


def _cmul_const(x, ang):
    c, s = _snap(math.cos(ang)), _snap(math.sin(ang))
    re = _sub(_scale(x[0], c), _scale(x[1], s))
    im = _add(_scale(x[0], s), _scale(x[1], c))
    return (re, im)


def _rfft_sym(x):
    n = len(x)
    if n == 1:
        return [(x[0], None)]
    if n == 2:
        return [(_add(x[0], x[1]), None), (_sub(x[0], x[1]), None)]
    ev = _rfft_sym(x[0::2])
    od = _rfft_sym(x[1::2])
    out = [None] * (n // 2 + 1)
    for k in range(n // 4 + 1):
        t = _cmul_const(od[k], -2.0 * math.pi * k / n)
        out[k] = _cadd(ev[k], t)
        if n // 2 - k != k:
            out[n // 2 - k] = _conj(_csub(ev[k], t))
    out[0] = (out[0][0], None)
    out[n // 2] = (out[n // 2][0], None)
    return out


def _irfft_sym(b, n, keep):
    if keep <= 0:
        return [None] * n
    if n == 1:
        return [b[0][0]]
    if n == 2:
        y1 = _sub(b[0][0], b[1][0]) if keep > 1 else None
        return [_add(b[0][0], b[1][0]), y1]
    b = list(b)
    b[0] = (b[0][0], None)
    b[n // 2] = (b[n // 2][0], None)
    ev, od = [], []
    for k in range(n // 4 + 1):
        cb = _conj(b[n // 2 - k])
        ev.append(_cadd(b[k], cb))
        od.append(_cmul_const(_csub(b[k], cb), 2.0 * math.pi * k / n))
    y = [None] * n
    y[0::2] = _irfft_sym(ev, n // 2, (keep + 1) // 2)
    y[1::2] = _irfft_sym(od, n // 2, keep // 2)
    return y


def _slow_forward(load_slab, n_slabs, a_ref, width):
    def body(i, carry):
        r = pl.multiple_of(i * ROWS, ROWS)
        xs = [load_slab(s, r) for s in range(n_slabs)] + [None] * (FFT_N1 - n_slabs)
        spec = _rfft_sym(xs)
        zero = jnp.zeros((ROWS, width), F32)
        for k1 in range(FFT_K1):
            re, im = spec[k1]
            a_ref[k1, pl.ds(r, ROWS), :] = zero if re is None else re
            a_ref[k1, pl.ds(FFT_N2 + r, ROWS), :] = zero if im is None else im
        return carry
    lax.fori_loop(0, FFT_N2 // ROWS, body, 0)


def _tile_lanes(t, width):
    reps = width // t.shape[-1]
    return t if reps == 1 else jnp.concatenate([t] * reps, axis=-1)


def _mod_kernel(c_ref, w_ref, b_ref, o_ref):
    c = c_ref[...]
    s = c * jax.nn.sigmoid(c)
    o_ref[...] = _dot(s, w_ref[...], HI) + b_ref[...]


def _mod_call(cc, w, b):
    n = w.shape[1]
    bn = 1024
    return pl.pallas_call(
        _mod_kernel,
        grid=(n // bn,),
        in_specs=[pl.BlockSpec(cc.shape, lambda j: (0, 0)),
                  pl.BlockSpec((D_MODEL, bn), lambda j: (0, j)),
                  pl.BlockSpec((1, bn), lambda j: (0, j))],
        out_specs=pl.BlockSpec((cc.shape[0], bn), lambda j: (0, j)),
        out_shape=jax.ShapeDtypeStruct((cc.shape[0], n), F32),
        compiler_params=pltpu.CompilerParams(vmem_limit_bytes=VMEM_LIMIT),
        name="mod",
    )(cc, w, b)


def _rms(x):
    return x * lax.rsqrt(jnp.mean(x * x, axis=-1, keepdims=True) + EPS)


def _short_conv(x, w, b):
    n = x.shape[0]
    pos = lax.broadcasted_iota(jnp.int32, x.shape, 0) & (GRID_W - 1)
    prev = jnp.where(pos == 0, 0.0, pltpu.roll(x, 1, 0))
    nxt = jnp.where(pos == GRID_W - 1, 0.0, pltpu.roll(x, n - 1, 0))
    return b + w[0:1, :] * prev + w[1:2, :] * x + w[2:3, :] * nxt


def _in_proj_kernel(*refs, latent):
    if latent:
        (x_ref, mod_ref, g_ref, w_ref, gw_ref, gb_ref, cw_ref, cb_ref,
         x0_ref, u_ref, q_ref, k_ref, v_ref, og_ref, lgf_ref, lgb_ref) = refs
    else:
        x_ref, mod_ref, g_ref, w_ref, gw_ref, gb_ref, k_ref, v_ref, lgf_ref, lgb_ref = refs
    x = x_ref[0]
    sh = mod_ref[0, :, 0:D_MODEL]
    sc = mod_ref[0, :, D_MODEL:2 * D_MODEL]
    h = ((_rms(x) * g_ref[...]) * (1.0 + sc) + sh).astype(BF16)
    c0 = 3 * D_HYENA
    c1 = c0 + GLA_DK
    c2 = c1 + GLA_DK
    c3 = c2 + GLA_DV
    if latent:
        dh = D_HYENA
        conv = [_short_conv(_dot(h, w_ref[:, j * dh:(j + 1) * dh]),
                            cw_ref[:, j * dh:(j + 1) * dh], cb_ref[:, j * dh:(j + 1) * dh])
                for j in range(3)]
        x0_ref[0] = conv[0].astype(BF16)
        u_ref[0] = (conv[1] * conv[2]).astype(BF16)
        q = _dot(h, w_ref[:, c0:c1]) * (GLA_HEAD_DK ** -0.5)
        q_ref[0] = q.astype(BF16)
        og_ref[0] = _dot(h, w_ref[:, c3:c3 + GLA_DV]).astype(BF16)
    k_ref[0] = _dot(h, w_ref[:, c1:c2]).astype(BF16)
    v_ref[0] = _dot(h, w_ref[:, c2:c3]).astype(BF16)
    gr = _dot(h, w_ref[:, GATE_COL:N_IN_PAD]).astype(BF16)
    z = _dot(gr, gw_ref[...]) + gb_ref[...]
    lg = jax.nn.log_sigmoid(z) * (1.0 / GATE_NORMALIZER)
    lgf_ref[0] = lg[:, 0:GLA_DK]
    lgb_ref[0] = lg[:, GLA_DK:2 * GLA_DK]


def _in_proj_call(x, mod, g, w, gw, gb, conv, tm):
    bsz, t, _ = x.shape
    latent = conv is not None
    def tile(width):
        return pl.BlockSpec((1, tm, width), lambda b, i: (b, i, 0))
    def whole(a):
        return pl.BlockSpec(a.shape, lambda b, i: (0,) * a.ndim)
    if latent:
        widths = (D_HYENA, D_HYENA, GLA_DK, GLA_DK, GLA_DV, GLA_DV, GLA_DK, GLA_DK)
        dtypes = (BF16,) * 6 + (F32, F32)
    else:
        widths = (GLA_DK, GLA_DV, GLA_DK, GLA_DK)
        dtypes = (BF16, BF16, F32, F32)
    args = [x, mod, g, w, gw, gb] + (list(conv) if latent else [])
    return pl.pallas_call(
        functools.partial(_in_proj_kernel, latent=latent),
        grid=(bsz, t // tm),
        in_specs=[tile(D_MODEL),
                  pl.BlockSpec((1, 1, 6 * D_MODEL), lambda b, i: (b, 0, 0))]
                 + [whole(a) for a in args[2:]],
        out_specs=[tile(wd) for wd in widths],
        out_shape=[jax.ShapeDtypeStruct((bsz, t, wd), dt) for wd, dt in zip(widths, dtypes)],
        compiler_params=pltpu.CompilerParams(
            dimension_semantics=("parallel", "parallel"), vmem_limit_bytes=VMEM_LIMIT),
        name="in_proj" if latent else "in_proj_ctx",
    )(*args)


def _ctx_state_kernel(k_ref, v_ref, lgf_ref, lgb_ref, sf_ref, sb_ref):
    t = k_ref.shape[1]
    row = lax.broadcasted_iota(jnp.int32, (t, t), 0)
    col = lax.broadcasted_iota(jnp.int32, (t, t), 1)
    k = k_ref[0].astype(F32)
    v = v_ref[0]
    for lg_ref, tri, o_ref in ((lgf_ref, col > row, sf_ref), (lgb_ref, col < row, sb_ref)):
        rem = _dot(tri.astype(F32), lg_ref[0], HI)
        kend = (k * jnp.exp(rem)).astype(BF16)
        ds = lax.dot_general(kend, v, (((0,), (0,)), ((), ())), preferred_element_type=F32)
        for h in range(GLA_HEADS):
            o_ref[0, h] = ds[h * GLA_HEAD_DK:(h + 1) * GLA_HEAD_DK,
                             h * GLA_HEAD_DV:(h + 1) * GLA_HEAD_DV]


def _ctx_state_call(k, v, lgf, lgb):
    bsz, t, _ = k.shape
    def tile(width):
        return pl.BlockSpec((1, t, width), lambda b: (b, 0, 0))
    st = pl.BlockSpec((1, GLA_HEADS, GLA_HEAD_DK, GLA_HEAD_DV), lambda b: (b, 0, 0, 0))
    return pl.pallas_call(
        _ctx_state_kernel,
        grid=(bsz,),
        in_specs=[tile(GLA_DK), tile(GLA_DV), tile(GLA_DK), tile(GLA_DK)],
        out_specs=[st, st],
        out_shape=[jax.ShapeDtypeStruct((bsz, GLA_HEADS, GLA_HEAD_DK, GLA_HEAD_DV), F32)] * 2,
        compiler_params=pltpu.CompilerParams(vmem_limit_bytes=VMEM_LIMIT),
        name="gla_ctx",
    )(k, v, lgf, lgb)


def _prefix_rows(x, reverse):
    n = x.shape[0]
    pos = lax.broadcasted_iota(jnp.int32, x.shape, 0) & (GLA_CHUNK - 1)
    k = 1
    while k < GLA_CHUNK:
        if reverse:
            x = x + jnp.where(pos < GLA_CHUNK - k, pltpu.roll(x, n - k, 0), 0.0)
        else:
            x = x + jnp.where(pos >= k, pltpu.roll(x, k, 0), 0.0)
        k *= 2
    return x


def _pair_diag(a, b):
    z = jnp.zeros_like(a)
    return jnp.concatenate([jnp.concatenate([a, z], axis=1),
                            jnp.concatenate([z, b], axis=1)], axis=0)


def _gla_chunk(q, k, v, b, s, reverse):
    c, hd, hv = GLA_CHUNK, GLA_HEAD_DK, GLA_HEAD_DV
    total = b[0:1] if reverse else b[c - 1:c]
    qin = (q * jnp.exp(b)).astype(BF16)
    kin = k * jnp.exp(-b)
    kend = k * jnp.exp(total - b)
    dec = jnp.exp(total)
    hi = dec.astype(BF16).astype(F32)
    mid = (dec - hi).astype(BF16).astype(F32)
    lo = dec - hi - mid
    r8 = lax.broadcasted_iota(jnp.int32, (8, GLA_DK), 0)
    extra = jnp.where(r8 == 0, hi, jnp.where(r8 == 1, mid, jnp.where(r8 == 2, lo, 0.0)))
    a = jnp.concatenate([kend, extra, jnp.zeros((c - 8, GLA_DK), F32)], axis=0)
    at = a.T.astype(BF16)
    r64 = lax.broadcasted_iota(jnp.int32, (c, 2 * hv), 0)
    l64 = lax.broadcasted_iota(jnp.int32, (c, 2 * hv), 1)
    ones_blk = jnp.where((r64 < 3) & (l64 >= hv), 1.0, 0.0).astype(BF16)
    zv = jnp.zeros((c, hv), BF16)
    key = lax.broadcasted_iota(jnp.int32, (c, 2 * hd), 1)
    qrow = lax.broadcasted_iota(jnp.int32, (c, 2 * hd), 0)
    att_mask = ((key & (hd - 1)) >= qrow) if reverse else ((key & (hd - 1)) <= qrow)
    outs, s_new = [], list(s)
    for p in range(GLA_HEADS // 2):
        lanes = slice(p * 2 * hd, (p + 1) * 2 * hd)
        q2, k2 = qin[:, lanes], kin[:, lanes]
        kbd = jnp.concatenate([jnp.where(key < hd, k2, 0.0), jnp.where(key >= hd, k2, 0.0)],
                              axis=0).astype(BF16)
        att = lax.dot_general(q2, kbd, (((1,), (1,)), ((), ())), preferred_element_type=F32)
        att = jnp.where(att_mask, att, 0.0).astype(BF16)
        h0, h1 = 2 * p, 2 * p + 1
        v0, v1 = v[:, h0 * hv:(h0 + 1) * hv], v[:, h1 * hv:(h1 + 1) * hv]
        rhs = jnp.concatenate([_pair_diag(v0, v1),
                               _pair_diag(s[h0].astype(BF16), s[h1].astype(BF16))], axis=0)
        outs.append(_dot(jnp.concatenate([att, q2], axis=1), rhs))
        for h, vh in ((h0, v0), (h1, v1)):
            inc = _dot(at[h * hd:(h + 1) * hd, :],
                       jnp.concatenate([jnp.concatenate([vh, zv], axis=1), ones_blk], axis=0))
            s_new[h] = inc[:, hv:] * s[h] + inc[:, :hv]
    return jnp.concatenate(outs, axis=1), s_new


def _gla_kernel(*refs, reverse):
    if reverse:
        q_ref, k_ref, v_ref, lg_ref, s0_ref, of_ref, og_ref, gn_ref, o_ref, s_scr, b_scr = refs
    else:
        q_ref, k_ref, v_ref, lg_ref, s0_ref, o_ref, s_scr, b_scr = refs
    c = GLA_CHUNK

    @pl.when(pl.program_id(1) == 0)
    def _():
        s_scr[...] = s0_ref[0]

    b_scr[...] = _prefix_rows(lg_ref[0], reverse)
    s = [s_scr[h] for h in range(GLA_HEADS)]
    n_chunks = q_ref.shape[1] // c
    order = range(n_chunks - 1, -1, -1) if reverse else range(n_chunks)
    for ci in order:
        rows = slice(ci * c, (ci + 1) * c)
        o, s = _gla_chunk(q_ref[0, rows, :].astype(F32), k_ref[0, rows, :].astype(F32),
                          v_ref[0, rows, :], b_scr[rows, :], s, reverse)
        if reverse:
            o = o + of_ref[0, rows, :]
            og = og_ref[0, rows, :].astype(F32)
            parts = []
            for h in range(GLA_HEADS):
                oh = o[:, h * GLA_HEAD_DV:(h + 1) * GLA_HEAD_DV]
                parts.append(_rms(oh) * gn_ref[...])
            y = jnp.concatenate(parts, axis=1) * (og * jax.nn.sigmoid(og))
            o_ref[0, rows, :] = y.astype(o_ref.dtype)
        else:
            o_ref[0, rows, :] = o
    for h in range(GLA_HEADS):
        s_scr[h] = s[h]


def _gla_call(q, k, v, lg, s0, extra, reverse, tt):
    bsz, t, _ = q.shape
    nt = t // tt
    if reverse:
        def tile(width):
            return pl.BlockSpec((1, tt, width), lambda b, j: (b, nt - 1 - j, 0))
    else:
        def tile(width):
            return pl.BlockSpec((1, tt, width), lambda b, j: (b, j, 0))
    in_specs = [tile(GLA_DK), tile(GLA_DK), tile(GLA_DV), tile(GLA_DK),
                pl.BlockSpec((1, GLA_HEADS, GLA_HEAD_DK, GLA_HEAD_DV), lambda b, j: (b, 0, 0, 0))]
    args = [q, k, v, lg, s0]
    if reverse:
        o_f, og, gn = extra
        in_specs += [tile(GLA_DV), tile(GLA_DV), pl.BlockSpec(gn.shape, lambda b, j: (0, 0))]
        args += [o_f, og, gn]
    return pl.pallas_call(
        functools.partial(_gla_kernel, reverse=reverse),
        grid=(bsz, nt),
        in_specs=in_specs,
        out_specs=tile(GLA_DV),
        out_shape=jax.ShapeDtypeStruct((bsz, t, GLA_DV), BF16 if reverse else F32),
        scratch_shapes=[pltpu.VMEM((GLA_HEADS, GLA_HEAD_DK, GLA_HEAD_DV), F32),
                        pltpu.VMEM((tt, GLA_DK), F32)],
        compiler_params=pltpu.CompilerParams(
            dimension_semantics=("parallel", "arbitrary"), vmem_limit_bytes=VMEM_LIMIT),
        name="gla_bwd" if reverse else "gla_fwd",
    )(*args)


def _split_bf16(x):
    hi = x.astype(BF16)
    return hi, (x - hi.astype(F32)).astype(BF16)


def _dot_split(a_hi, a_lo, b_hi, b_lo):
    return _dot(a_hi, b_hi) + _dot(a_hi, b_lo) + _dot(a_lo, b_hi)


def _filter_mlp_kernel(z_ref, w1_ref, b1_ref, w2_ref, b2_ref, w3_ref, b3_ref, fr_ref, o_ref):
    fr = fr_ref[...]
    h = jnp.sin(fr * (_dot(z_ref[...], w1_ref[...], HI) + b1_ref[...]))
    h = jnp.sin(fr * (_dot(h, w2_ref[...], HI) + b2_ref[...]))
    o_ref[...] = jnp.sin(fr * (_dot(h, w3_ref[...], HI) + b3_ref[...]))


def _filter_mlp_call(zp, w1, b1, w2, b2, w3, b3, fr):
    return pl.pallas_call(
        _filter_mlp_kernel,
        out_shape=jax.ShapeDtypeStruct((zp.shape[0], 2 * FILTER_HIDDEN), F32),
        compiler_params=pltpu.CompilerParams(vmem_limit_bytes=VMEM_LIMIT),
        name="filter_mlp",
    )(zp, w1, b1, w2, b2, w3, b3, fr)


def _filter_kernel(hdn_ref, wfh_ref, wfl_ref, wbh_ref, wbl_ref, t_ref, dl_ref, f2h_ref, f2l_ref,
                   twr_ref, twi_ref, khat_ref, kern_ref, a_ref):
    dl = dl_ref[...]
    blk = 512

    def taps_body(i, ss):
        r0 = pl.multiple_of(i * blk, blk)
        rows = pl.ds(r0, blk)
        hf = _dot_split(*_split_bf16(hdn_ref[0, rows, :]), wfh_ref[...], wfl_ref[...])
        hb = _dot_split(*_split_bf16(hdn_ref[1, rows, :]), wbh_ref[...], wbl_ref[...])
        hf = hf * jnp.exp(-_tile_lanes(t_ref[0, rows, :], CG) * dl)
        hb = hb * jnp.exp(-_tile_lanes(t_ref[1, rows, :], CG) * dl)
        lag = r0 + lax.broadcasted_iota(jnp.int32, hb.shape, 0)
        hb = jnp.where(lag == 0, 0.0, hb)
        kern_ref[rows, :] = hf
        kern_ref[pl.ds(SEQ + r0, blk), :] = hb
        return (ss + jnp.sum(hf * hf, axis=0, keepdims=True)
                + jnp.sum(hb * hb, axis=0, keepdims=True))
    ss = lax.fori_loop(0, SEQ // blk, taps_body, jnp.zeros((1, CG), F32))
    scale = lax.rsqrt(ss + EPS) * (1.0 / FFT_N)

    def load_slab(s, r):
        return kern_ref[pl.ds(s * FFT_N2 + r, ROWS), :]
    _slow_forward(load_slab, FFT_N1, a_ref, CG)

    def body(k1, carry):
        ar = a_ref[k1, 0:FFT_N2, :]
        ai = a_ref[k1, FFT_N2:2 * FFT_N2, :]
        twr = _tile_lanes(twr_ref[k1], CG)
        twi = _tile_lanes(twi_ref[k1], CG)
        a = jnp.concatenate([ar * twr - ai * twi, ar * twi + ai * twr], axis=0)
        x = _dot_split(f2h_ref[...], f2l_ref[...], *_split_bf16(a))
        khat_ref[k1] = x * scale
        return carry
    lax.fori_loop(0, FFT_K1, body, 0)


def _filter_call(hdn, wout_hi, wout_lo, t2, deltas, f2_hi, f2_lo, twr, twi):
    ng = D_HYENA // CG
    def whole(a):
        return pl.BlockSpec(a.shape, lambda g: (0,) * a.ndim)
    fwd_cols = pl.BlockSpec((FILTER_HIDDEN, CG), lambda g: (0, g))
    bwd_cols = pl.BlockSpec((FILTER_HIDDEN, CG), lambda g: (0, ng + g))
    return pl.pallas_call(
        _filter_kernel,
        grid=(ng,),
        in_specs=[whole(hdn), fwd_cols, fwd_cols, bwd_cols, bwd_cols,
                  whole(t2),
                  pl.BlockSpec((1, CG), lambda g: (0, g)),
                  whole(f2_hi), whole(f2_lo), whole(twr), whole(twi)],
        out_specs=pl.BlockSpec((FFT_K1, 2 * FFT_N2, CG), lambda g: (0, 0, g)),
        out_shape=jax.ShapeDtypeStruct((FFT_K1, 2 * FFT_N2, D_HYENA), F32),
        scratch_shapes=[pltpu.VMEM((FFT_N, CG), F32),
                        pltpu.VMEM((FFT_K1, 2 * FFT_N2, CG), F32)],
        compiler_params=pltpu.CompilerParams(vmem_limit_bytes=VMEM_LIMIT),
        name="filter_spec",
    )(hdn, wout_hi, wout_lo, wout_hi, wout_lo, t2, deltas, f2_hi, f2_lo, twr, twi)


def _freq_step(k1, a_ref, khat_ref, ff_ref, fi_ref, twr_ref, twi_ref):
    ar = a_ref[k1, 0:FFT_N2, :]
    ai = a_ref[k1, FFT_N2:2 * FFT_N2, :]
    twr = _tile_lanes(twr_ref[k1], CG)
    twi = _tile_lanes(twi_ref[k1], CG)
    a = jnp.concatenate([ar * twr - ai * twi, ar * twi + ai * twr], axis=0).astype(BF16)
    x = _dot(ff_ref[...], a)
    xr, xi = x[0:FFT_N2], x[FFT_N2:]
    kr = khat_ref[k1, 0:FFT_N2, :]
    ki = khat_ref[k1, FFT_N2:2 * FFT_N2, :]
    y = jnp.concatenate([xr * kr - xi * ki, xr * ki + xi * kr], axis=0).astype(BF16)
    bq = _dot(fi_ref[...], y)
    br, bi = bq[0:FFT_N2], bq[FFT_N2:]
    a_ref[k1, 0:FFT_N2, :] = br * twr + bi * twi
    a_ref[k1, FFT_N2:2 * FFT_N2, :] = bi * twr - br * twi


def _hyena_kernel(x0_ref, uin_ref, bias_ref, khat_ref, ff_ref, fi_ref, twr_ref, twi_ref,
                  o_ref, u_ref, a_ref):
    blk = FFT_N2

    def load_body(i, carry):
        rows = pl.ds(pl.multiple_of(i * blk, blk), blk)
        u_ref[rows, :] = uin_ref[0, rows, :].astype(F32)
        return carry
    lax.fori_loop(0, SEQ // blk, load_body, 0)

    def load_slab(s, r):
        return u_ref[pl.ds(s * FFT_N2 + r, ROWS), :]
    _slow_forward(load_slab, FFT_N1 // 2, a_ref, CG)

    def freq_body(i, carry):
        for k1 in (2 * i, 2 * i + 1):
            _freq_step(k1, a_ref, khat_ref, ff_ref, fi_ref, twr_ref, twi_ref)
        return carry
    lax.fori_loop(0, FFT_K1 // 2, freq_body, 0)
    _freq_step(FFT_K1 - 1, a_ref, khat_ref, ff_ref, fi_ref, twr_ref, twi_ref)

    bias = bias_ref[...]

    def inv_body(i, carry):
        r = pl.multiple_of(i * ROWS, ROWS)
        spec = [(a_ref[k1, pl.ds(r, ROWS), :], a_ref[k1, pl.ds(FFT_N2 + r, ROWS), :])
                for k1 in range(FFT_K1)]
        ys = _irfft_sym(spec, FFT_N1, FFT_N1 // 2)
        for s in range(FFT_N1 // 2):
            rows = pl.ds(s * FFT_N2 + r, ROWS)
            u_ref[rows, :] = ys[s] + bias * u_ref[rows, :]
        return carry
    lax.fori_loop(0, FFT_N2 // ROWS, inv_body, 0)

    def out_body(i, carry):
        rows = pl.ds(pl.multiple_of(i * blk, blk), blk)
        o_ref[0, rows, :] = (x0_ref[0, rows, :].astype(F32) * u_ref[rows, :]).astype(o_ref.dtype)
        return carry
    lax.fori_loop(0, SEQ // blk, out_body, 0)


def _hyena_call(x0c, u, bias, khat, ff, fi, twr, twi):
    bsz = u.shape[0]
    ng = D_HYENA // CG
    stream = pl.BlockSpec((1, SEQ, CG), lambda g, b: (b, 0, g))
    def whole(a):
        return pl.BlockSpec(a.shape, lambda g, b: (0,) * a.ndim)
    return pl.pallas_call(
        _hyena_kernel,
        grid=(ng, bsz),
        in_specs=[stream, stream,
                  pl.BlockSpec((1, CG), lambda g, b: (0, g)),
                  pl.BlockSpec((FFT_K1, 2 * FFT_N2, CG), lambda g, b: (0, 0, g)),
                  whole(ff), whole(fi), whole(twr), whole(twi)],
        out_specs=stream,
        out_shape=jax.ShapeDtypeStruct((bsz, SEQ, D_HYENA), BF16),
        scratch_shapes=[pltpu.VMEM((SEQ, CG), F32),
                        pltpu.VMEM((FFT_K1, 2 * FFT_N2, CG), F32)],
        compiler_params=pltpu.CompilerParams(
            dimension_semantics=("parallel", "parallel"), vmem_limit_bytes=VMEM_LIMIT),
        name="hyena",
    )(x0c, u, bias, khat, ff, fi, twr, twi)


def _out_kernel(x_ref, yh_ref, yg_ref, mod_ref, wo_ref, nm_ref, w1_ref, w2_ref, nf_ref, o_ref):
    d = D_MODEL
    g1 = mod_ref[0, :, 2 * d:3 * d]
    sh2 = mod_ref[0, :, 3 * d:4 * d]
    sc2 = mod_ref[0, :, 4 * d:5 * d]
    g2 = mod_ref[0, :, 5 * d:6 * d]
    mix = _dot(yh_ref[0], wo_ref[0:D_HYENA, :]) + _dot(yg_ref[0], wo_ref[D_HYENA:d, :])
    x1 = x_ref[0] + g1 * mix
    h2 = ((_rms(x1) * nm_ref[...]) * (1.0 + sc2) + sh2).astype(BF16)
    acc = jnp.zeros(x1.shape, F32)
    for j in range(D_FF // d):
        hid = _dot(h2, w1_ref[:, j * d:(j + 1) * d])
        hid = jnp.square(jnp.maximum(hid, 0.0)).astype(BF16)
        acc = acc + _dot(hid, w2_ref[j * d:(j + 1) * d, :])
    x2 = x1 + g2 * acc
    o_ref[0] = _rms(x2) * nf_ref[...]


def _out_call(x, yh, yg, mod, wo, nm, w1, w2, nf, tm):
    bsz, t, _ = x.shape
    def tile(width):
        return pl.BlockSpec((1, tm, width), lambda b, i: (b, i, 0))
    def whole(a):
        return pl.BlockSpec(a.shape, lambda b, i: (0,) * a.ndim)
    return pl.pallas_call(
        _out_kernel,
        grid=(bsz, t // tm),
        in_specs=[tile(D_MODEL), tile(D_HYENA), tile(GLA_DV),
                  pl.BlockSpec((1, 1, 6 * D_MODEL), lambda b, i: (b, 0, 0)),
                  whole(wo), whole(nm), whole(w1), whole(w2), whole(nf)],
        out_specs=tile(D_MODEL),
        out_shape=jax.ShapeDtypeStruct((bsz, t, D_MODEL), F32),
        compiler_params=pltpu.CompilerParams(
            dimension_semantics=("parallel", "parallel"), vmem_limit_bytes=VMEM_LIMIT),
        name="out_mlp",
    )(x, yh, yg, mod, wo, nm, w1, w2, nf)


def _lag_reverse(a):
    return jnp.concatenate([a[0:1], a[:0:-1]], axis=0)


def _pair(w):
    z = jnp.zeros_like(w)
    return jnp.block([[w, z], [z, w]])


def _filter_features():
    length = SEQ
    t = jnp.linspace(0.0, 1.0, length, dtype=F32)[:, None]
    w = 2.0 * math.pi * jnp.arange(length, dtype=F32) / length
    f = jnp.linspace(1e-4, FILTER_BANDS - 1, FILTER_BANDS, dtype=F32)
    ang = w[:, None] * f[None, :]
    z = jnp.concatenate([t, jnp.cos(ang), -jnp.sin(ang)], axis=-1)
    z = jnp.pad(z, ((0, 0), (0, 128 - FILTER_EMB)))
    zp = jnp.concatenate([z[:length // 2], z[length // 2:]], axis=1)
    t2 = jnp.broadcast_to(jnp.stack([t, _lag_reverse(t)]), (2, length, 128))
    min_decay = math.log(DECAY_TARGET) / SLOW_DECAY_PCT
    max_decay = math.log(DECAY_TARGET) / FAST_DECAY_PCT
    deltas = jnp.abs(jnp.linspace(min_decay, max_decay, D_HYENA, dtype=F32))[None, :]
    return zp, t2, deltas


def _dft_tables():
    n2 = jnp.arange(FFT_N2, dtype=jnp.int32)
    k1 = jnp.arange(FFT_K1, dtype=jnp.int32)
    ang_tw = (2.0 * math.pi / FFT_N) * (k1[:, None] * n2[None, :]).astype(F32)
    twr = jnp.broadcast_to(jnp.cos(ang_tw)[:, :, None], (FFT_K1, FFT_N2, 128))
    twi = jnp.broadcast_to(-jnp.sin(ang_tw)[:, :, None], (FFT_K1, FFT_N2, 128))
    ang = (2.0 * math.pi / FFT_N2) * ((n2[:, None] * n2[None, :]) % FFT_N2).astype(F32)
    cr, ci = jnp.cos(ang), -jnp.sin(ang)
    fwd = jnp.block([[cr, -ci], [ci, cr]])
    inv = jnp.block([[cr, ci], [-ci, cr]])
    return fwd, inv, twr, twi


def kernel(x, c, ctx, c_ctx, w_ada, b_ada, norm_mix, norm_mlp, w_in, conv_w, conv_b, filt_w1, filt_b1, filt_w2, filt_b2, filt_w3, filt_b3, filt_freq, filt_wout, hyena_bias, gk_w_fwd, gk_b_fwd, gk_w_bwd, gk_b_bwd, gla_norm, w_out, w_mlp1, w_mlp2, norm_final):
    bsz = x.shape[0]
    l = 0
    cc = jnp.concatenate([c, c_ctx[None, :], jnp.zeros((16 - bsz - 1, D_MODEL), F32)], axis=0)
    mod_all = _mod_call(cc, w_ada[l], b_ada[l][None, :])
    mod = mod_all[:bsz][:, None, :]
    mod_c = jnp.broadcast_to(mod_all[bsz][None, None, :], (bsz, 1, 6 * D_MODEL))

    w_in_p = jnp.pad(w_in[l], ((0, 0), (0, N_IN_PAD - N_IN))).astype(BF16)
    gw = jnp.zeros((N_IN_PAD - GATE_COL, 2 * GLA_DK), F32)
    gw = gw.at[0:GATE_RANK, 0:GLA_DK].set(gk_w_fwd[l])
    gw = gw.at[GATE_RANK:2 * GATE_RANK, GLA_DK:].set(gk_w_bwd[l])
    gb = jnp.concatenate([gk_b_fwd[l], gk_b_bwd[l]])[None, :]
    gw = gw.astype(BF16)
    g_mix = norm_mix[l][None, :]

    conv = (conv_w[l], conv_b[l][None, :])
    x0c, u, q, k, v, og, lgf, lgb = _in_proj_call(x, mod, g_mix, w_in_p, gw, gb, conv, 512)
    k_c, v_c, lgf_c, lgb_c = _in_proj_call(ctx, mod_c, g_mix, w_in_p, gw, gb, None, ctx.shape[1])

    s_f, s_b = _ctx_state_call(k_c, v_c, lgf_c, lgb_c)
    o_f = _gla_call(q, k, v, lgf, s_f, None, False, 512)
    y_gla = _gla_call(q, k, v, lgb, s_b, (o_f, og, gla_norm[l][None, :]), True, 512)

    zp, t2, deltas = _filter_features()
    f2_fwd, f2_inv, twr, twi = _dft_tables()
    def two(b):
        return jnp.concatenate([b, b])[None, :]
    w1p = jnp.pad(filt_w1[l], ((0, 128 - FILTER_EMB), (0, 0)))
    hp = _filter_mlp_call(zp, _pair(w1p), two(filt_b1[l]), _pair(filt_w2[l]), two(filt_b2[l]),
                          _pair(filt_w3[l]), two(filt_b3[l]), two(filt_freq[l]))
    hdn = jnp.concatenate([hp[:, :FILTER_HIDDEN], hp[:, FILTER_HIDDEN:]], axis=0)
    hdn2 = jnp.stack([hdn, _lag_reverse(hdn)])
    khat = _filter_call(hdn2, *_split_bf16(filt_wout[l]), t2, deltas, *_split_bf16(f2_fwd), twr, twi)
    y_hy = _hyena_call(x0c, u, hyena_bias[l][None, :], khat,
                       f2_fwd.astype(BF16), f2_inv.astype(BF16), twr, twi)

    return _out_call(x, y_hy, y_gla, mod, w_out[l].astype(BF16), norm_mlp[l][None, :],
                     w_mlp1[l].astype(BF16), w_mlp2[l].astype(BF16), norm_final[None, :], 512)
```

```python
import functools
import math

import jax
import jax.numpy as jnp
from jax import lax
from jax.experimental import pallas as pl
from jax.experimental.pallas import tpu as pltpu

D_MODEL = 1024
SEQ = 4096
GRID_W = 64
D_HYENA = 512
GLA_HEADS = 4
GLA_DK = 256
GLA_DV = 512
GLA_HEAD_DK = 64
GLA_HEAD_DV = 128
GATE_RANK = 16
GATE_NORMALIZER = 16.0
GLA_CHUNK = 64
FILTER_EMB = 33
FILTER_BANDS = 16
FILTER_HIDDEN = 64
FAST_DECAY_PCT = 0.3
SLOW_DECAY_PCT = 1.5
DECAY_TARGET = 1e-2
D_FF = 4 * D_MODEL
EPS = 1e-6
N_IN = 3104
N_IN_PAD = 3200
GATE_COL = 3072

FFT_N = 2 * SEQ
FFT_N1 = 32
FFT_N2 = 256
FFT_K1 = FFT_N1 // 2 + 1
CG = 256
ROWS = 8

F32 = jnp.float32
BF16 = jnp.bfloat16
HI = lax.Precision.HIGHEST
VMEM_LIMIT = 56 * 1024 * 1024


def _dot(a, b, precision=None):
    return jnp.dot(a, b, preferred_element_type=F32, precision=precision)


def _add(a, b):
    if a is None:
        return b
    if b is None:
        return a
    return a + b


def _sub(a, b):
    if b is None:
        return a
    if a is None:
        return -b
    return a - b


def _scale(a, c):
    if a is None or c == 0.0:
        return None
    if c == 1.0:
        return a
    if c == -1.0:
        return -a
    return a * c


def _cadd(x, y):
    return (_add(x[0], y[0]), _add(x[1], y[1]))


def _csub(x, y):
    return (_sub(x[0], y[0]), _sub(x[1], y[1]))


def _conj(x):
    return (x[0], None if x[1] is None else -x[1])


def _snap(v):
    for t in (0.0, 1.0, -1.0):
        if abs(v - t) < 1e-12:
            return t
    return v


def _cmul_const(x, ang):
    c, s = _snap(math.cos(ang)), _snap(math.sin(ang))
    re = _sub(_scale(x[0], c), _scale(x[1], s))
    im = _add(_scale(x[0], s), _scale(x[1], c))
    return (re, im)


def _rfft_sym(x):
    n = len(x)
    if n == 1:
        return [(x[0], None)]
    if n == 2:
        return [(_add(x[0], x[1]), None), (_sub(x[0], x[1]), None)]
    ev = _rfft_sym(x[0::2])
    od = _rfft_sym(x[1::2])
    out = [None] * (n // 2 + 1)
    for k in range(n // 4 + 1):
        t = _cmul_const(od[k], -2.0 * math.pi * k / n)
        out[k] = _cadd(ev[k], t)
        if n // 2 - k != k:
            out[n // 2 - k] = _conj(_csub(ev[k], t))
    out[0] = (out[0][0], None)
    out[n // 2] = (out[n // 2][0], None)
    return out


def _irfft_sym(b, n, keep):
    if keep <= 0:
        return [None] * n
    if n == 1:
        return [b[0][0]]
    if n == 2:
        y1 = _sub(b[0][0], b[1][0]) if keep > 1 else None
        return [_add(b[0][0], b[1][0]), y1]
    b = list(b)
    b[0] = (b[0][0], None)
    b[n // 2] = (b[n // 2][0], None)
    ev, od = [], []
    for k in range(n // 4 + 1):
        cb = _conj(b[n // 2 - k])
        ev.append(_cadd(b[k], cb))
        od.append(_cmul_const(_csub(b[k], cb), 2.0 * math.pi * k / n))
    y = [None] * n
    y[0::2] = _irfft_sym(ev, n // 2, (keep + 1) // 2)
    y[1::2] = _irfft_sym(od, n // 2, keep // 2)
    return y


def _slow_forward(load_slab, n_slabs, a_ref, width):
    def body(i, carry):
        r = pl.multiple_of(i * ROWS, ROWS)
        xs = [load_slab(s, r) for s in range(n_slabs)] + [None] * (FFT_N1 - n_slabs)
        spec = _rfft_sym(xs)
        zero = jnp.zeros((ROWS, width), F32)
        for k1 in range(FFT_K1):
            re, im = spec[k1]
            a_ref[k1, pl.ds(r, ROWS), :] = zero if re is None else re
            a_ref[k1, pl.ds(FFT_N2 + r, ROWS), :] = zero if im is None else im
        return carry
    lax.fori_loop(0, FFT_N2 // ROWS, body, 0)


def _tile_lanes(t, width):
    reps = width // t.shape[-1]
    return t if reps == 1 else jnp.concatenate([t] * reps, axis=-1)


def _mod_kernel(c_ref, w_ref, b_ref, o_ref):
    c = c_ref[...]
    s = c * jax.nn.sigmoid(c)
    o_ref[...] = _dot(s, w_ref[...], HI) + b_ref[...]


def _mod_call(cc, w, b):
    n = w.shape[1]
    bn = 1024
    return pl.pallas_call(
        _mod_kernel,
        grid=(n // bn,),
        in_specs=[pl.BlockSpec(cc.shape, lambda j: (0, 0)),
                  pl.BlockSpec((D_MODEL, bn), lambda j: (0, j)),
                  pl.BlockSpec((1, bn), lambda j: (0, j))],
        out_specs=pl.BlockSpec((cc.shape[0], bn), lambda j: (0, j)),
        out_shape=jax.ShapeDtypeStruct((cc.shape[0], n), F32),
        compiler_params=pltpu.CompilerParams(vmem_limit_bytes=VMEM_LIMIT),
        name="mod",
    )(cc, w, b)


def _rms(x):
    return x * lax.rsqrt(jnp.mean(x * x, axis=-1, keepdims=True) + EPS)


def _short_conv(x, w, b):
    n = x.shape[0]
    pos = lax.broadcasted_iota(jnp.int32, x.shape, 0) & (GRID_W - 1)
    prev = jnp.where(pos == 0, 0.0, pltpu.roll(x, 1, 0))
    nxt = jnp.where(pos == GRID_W - 1, 0.0, pltpu.roll(x, n - 1, 0))
    return b + w[0:1, :] * prev + w[1:2, :] * x + w[2:3, :] * nxt


def _in_proj_kernel(*refs, latent):
    if latent:
        (x_ref, mod_ref, g_ref, w_ref, gw_ref, gb_ref, cw_ref, cb_ref,
         x0_ref, u_ref, q_ref, k_ref, v_ref, og_ref, lgf_ref, lgb_ref) = refs
    else:
        x_ref, mod_ref, g_ref, w_ref, gw_ref, gb_ref, k_ref, v_ref, lgf_ref, lgb_ref = refs
    x = x_ref[0]
    sh = mod_ref[0, :, 0:D_MODEL]
    sc = mod_ref[0, :, D_MODEL:2 * D_MODEL]
    h = ((_rms(x) * g_ref[...]) * (1.0 + sc) + sh).astype(BF16)
    c0 = 3 * D_HYENA
    c1 = c0 + GLA_DK
    c2 = c1 + GLA_DK
    c3 = c2 + GLA_DV
    if latent:
        dh = D_HYENA
        conv = [_short_conv(_dot(h, w_ref[:, j * dh:(j + 1) * dh]),
                            cw_ref[:, j * dh:(j + 1) * dh], cb_ref[:, j * dh:(j + 1) * dh])
                for j in range(3)]
        x0_ref[0] = conv[0].astype(BF16)
        u_ref[0] = (conv[1] * conv[2]).astype(BF16)
        q = _dot(h, w_ref[:, c0:c1]) * (GLA_HEAD_DK ** -0.5)
        q_ref[0] = q.astype(BF16)
        og_ref[0] = _dot(h, w_ref[:, c3:c3 + GLA_DV]).astype(BF16)
    k_ref[0] = _dot(h, w_ref[:, c1:c2]).astype(BF16)
    v_ref[0] = _dot(h, w_ref[:, c2:c3]).astype(BF16)
    gr = _dot(h, w_ref[:, GATE_COL:N_IN_PAD]).astype(BF16)
    z = _dot(gr, gw_ref[...]) + gb_ref[...]
    lg = jax.nn.log_sigmoid(z) * (1.0 / GATE_NORMALIZER)
    lgf_ref[0] = lg[:, 0:GLA_DK]
    lgb_ref[0] = lg[:, GLA_DK:2 * GLA_DK]


def _in_proj_call(x, mod, g, w, gw, gb, conv, tm):
    bsz, t, _ = x.shape
    latent = conv is not None
    def tile(width):
        return pl.BlockSpec((1, tm, width), lambda b, i: (b, i, 0))
    def whole(a):
        return pl.BlockSpec(a.shape, lambda b, i: (0,) * a.ndim)
    if latent:
        widths = (D_HYENA, D_HYENA, GLA_DK, GLA_DK, GLA_DV, GLA_DV, GLA_DK, GLA_DK)
        dtypes = (BF16,) * 6 + (F32, F32)
    else:
        widths = (GLA_DK, GLA_DV, GLA_DK, GLA_DK)
        dtypes = (BF16, BF16, F32, F32)
    args = [x, mod, g, w, gw, gb] + (list(conv) if latent else [])
    return pl.pallas_call(
        functools.partial(_in_proj_kernel, latent=latent),
        grid=(bsz, t // tm),
        in_specs=[tile(D_MODEL),
                  pl.BlockSpec((1, 1, 6 * D_MODEL), lambda b, i: (b, 0, 0))]
                 + [whole(a) for a in args[2:]],
        out_specs=[tile(wd) for wd in widths],
        out_shape=[jax.ShapeDtypeStruct((bsz, t, wd), dt) for wd, dt in zip(widths, dtypes)],
        compiler_params=pltpu.CompilerParams(
            dimension_semantics=("parallel", "parallel"), vmem_limit_bytes=VMEM_LIMIT),
        name="in_proj" if latent else "in_proj_ctx",
    )(*args)


def _ctx_state_kernel(k_ref, v_ref, lgf_ref, lgb_ref, sf_ref, sb_ref):
    t = k_ref.shape[1]
    row = lax.broadcasted_iota(jnp.int32, (t, t), 0)
    col = lax.broadcasted_iota(jnp.int32, (t, t), 1)
    k = k_ref[0].astype(F32)
    v = v_ref[0]
    for lg_ref, tri, o_ref in ((lgf_ref, col > row, sf_ref), (lgb_ref, col < row, sb_ref)):
        rem = _dot(tri.astype(F32), lg_ref[0], HI)
        kend = (k * jnp.exp(rem)).astype(BF16)
        ds = lax.dot_general(kend, v, (((0,), (0,)), ((), ())), preferred_element_type=F32)
        for h in range(GLA_HEADS):
            o_ref[0, h] = ds[h * GLA_HEAD_DK:(h + 1) * GLA_HEAD_DK,
                             h * GLA_HEAD_DV:(h + 1) * GLA_HEAD_DV]


def _ctx_state_call(k, v, lgf, lgb):
    bsz, t, _ = k.shape
    def tile(width):
        return pl.BlockSpec((1, t, width), lambda b: (b, 0, 0))
    st = pl.BlockSpec((1, GLA_HEADS, GLA_HEAD_DK, GLA_HEAD_DV), lambda b: (b, 0, 0, 0))
    return pl.pallas_call(
        _ctx_state_kernel,
        grid=(bsz,),
        in_specs=[tile(GLA_DK), tile(GLA_DV), tile(GLA_DK), tile(GLA_DK)],
        out_specs=[st, st],
        out_shape=[jax.ShapeDtypeStruct((bsz, GLA_HEADS, GLA_HEAD_DK, GLA_HEAD_DV), F32)] * 2,
        compiler_params=pltpu.CompilerParams(vmem_limit_bytes=VMEM_LIMIT),
        name="gla_ctx",
    )(k, v, lgf, lgb)


def _prefix_rows(x, reverse):
    n = x.shape[0]
    pos = lax.broadcasted_iota(jnp.int32, x.shape, 0) & (GLA_CHUNK - 1)
    k = 1
    while k < GLA_CHUNK:
        if reverse:
            x = x + jnp.where(pos < GLA_CHUNK - k, pltpu.roll(x, n - k, 0), 0.0)
        else:
            x = x + jnp.where(pos >= k, pltpu.roll(x, k, 0), 0.0)
        k *= 2
    return x


def _pair_diag(a, b):
    z = jnp.zeros_like(a)
    return jnp.concatenate([jnp.concatenate([a, z], axis=1),
                            jnp.concatenate([z, b], axis=1)], axis=0)


def _gla_stage1(q, k, v, b, reverse):
    c, hd, hv = GLA_CHUNK, GLA_HEAD_DK, GLA_HEAD_DV
    total = b[0:1] if reverse else b[c - 1:c]
    qin = (q * jnp.exp(b)).astype(BF16)
    kin = k * jnp.exp(-b)
    kend = k * jnp.exp(total - b)
    dec = jnp.exp(total)
    hi = dec.astype(BF16).astype(F32)
    mid = (dec - hi).astype(BF16).astype(F32)
    lo = dec - hi - mid
    r8 = lax.broadcasted_iota(jnp.int32, (8, GLA_DK), 0)
    extra = jnp.where(r8 == 0, hi, jnp.where(r8 == 1, mid, jnp.where(r8 == 2, lo, 0.0)))
    a = jnp.concatenate([kend, extra, jnp.zeros((c - 8, GLA_DK), F32)], axis=0)
    at = a.T.astype(BF16)
    r64 = lax.broadcasted_iota(jnp.int32, (c, 2 * hv), 0)
    l64 = lax.broadcasted_iota(jnp.int32, (c, 2 * hv), 1)
    ones_blk = jnp.where((r64 < 3) & (l64 >= hv), 1.0, 0.0).astype(BF16)
    zv = jnp.zeros((c, hv), BF16)
    key = lax.broadcasted_iota(jnp.int32, (c, 2 * hd), 1)
    qrow = lax.broadcasted_iota(jnp.int32, (c, 2 * hd), 0)
    att_mask = ((key & (hd - 1)) >= qrow) if reverse else ((key & (hd - 1)) <= qrow)
    atts = []
    for p in range(GLA_HEADS // 2):
        lanes = slice(p * 2 * hd, (p + 1) * 2 * hd)
        k2 = kin[:, lanes]
        kbd = jnp.concatenate([jnp.where(key < hd, k2, 0.0), jnp.where(key >= hd, k2, 0.0)],
                              axis=0).astype(BF16)
        att = lax.dot_general(qin[:, lanes], kbd, (((1,), (1,)), ((), ())),
                              preferred_element_type=F32)
        atts.append(jnp.where(att_mask, att, 0.0).astype(BF16))
    incs = []
    for h in range(GLA_HEADS):
        vh = v[:, h * hv:(h + 1) * hv]
        incs.append(_dot(at[h * hd:(h + 1) * hd, :],
                         jnp.concatenate([jnp.concatenate([vh, zv], axis=1), ones_blk], axis=0)))
    return qin, atts, incs


def _gla_stage2(qin, atts, incs, v, s):
    hd, hv = GLA_HEAD_DK, GLA_HEAD_DV
    outs = []
    for p in range(GLA_HEADS // 2):
        h0, h1 = 2 * p, 2 * p + 1
        rhs = jnp.concatenate([_pair_diag(v[:, h0 * hv:(h0 + 1) * hv], v[:, h1 * hv:(h1 + 1) * hv]),
                               _pair_diag(s[h0].astype(BF16), s[h1].astype(BF16))], axis=0)
        lhs = jnp.concatenate([atts[p], qin[:, p * 2 * hd:(p + 1) * 2 * hd]], axis=1)
        outs.append(_dot(lhs, rhs))
    s_new = [inc[:, hv:] * sh + inc[:, :hv] for inc, sh in zip(incs, s)]
    return jnp.concatenate(outs, axis=1), s_new


def _gla_kernel(*refs, reverse):
    if reverse:
        q_ref, k_ref, v_ref, lg_ref, s0_ref, of_ref, og_ref, gn_ref, o_ref, s_scr, b_scr = refs
    else:
        q_ref, k_ref, v_ref, lg_ref, s0_ref, o_ref, s_scr, b_scr = refs
    c = GLA_CHUNK

    @pl.when(pl.program_id(1) == 0)
    def _():
        s_scr[...] = s0_ref[0]

    b_scr[...] = _prefix_rows(lg_ref[0], reverse)
    n_chunks = q_ref.shape[1] // c
    order = list(range(n_chunks - 1, -1, -1) if reverse else range(n_chunks))

    def start(ci):
        rows = slice(ci * c, (ci + 1) * c)
        return _gla_stage1(q_ref[0, rows, :].astype(F32), k_ref[0, rows, :].astype(F32),
                           v_ref[0, rows, :], b_scr[rows, :], reverse)

    def finish(ci, staged, s):
        rows = slice(ci * c, (ci + 1) * c)
        o, s = _gla_stage2(*staged, v_ref[0, rows, :], s)
        if reverse:
            o = o + of_ref[0, rows, :]
            og = og_ref[0, rows, :].astype(F32)
            parts = []
            for h in range(GLA_HEADS):
                oh = o[:, h * GLA_HEAD_DV:(h + 1) * GLA_HEAD_DV]
                parts.append(_rms(oh) * gn_ref[...])
            y = jnp.concatenate(parts, axis=1) * (og * jax.nn.sigmoid(og))
            o_ref[0, rows, :] = y.astype(o_ref.dtype)
        else:
            o_ref[0, rows, :] = o
        return s

    s = [s_scr[h] for h in range(GLA_HEADS)]
    staged = start(order[0])
    for prev, ci in zip(order, order[1:]):
        nxt = start(ci)
        s = finish(prev, staged, s)
        staged = nxt
    s = finish(order[-1], staged, s)
    for h in range(GLA_HEADS):
        s_scr[h] = s[h]


def _gla_call(q, k, v, lg, s0, extra, reverse, tt):
    bsz, t, _ = q.shape
    nt = t // tt
    if reverse:
        def tile(width):
            return pl.BlockSpec((1, tt, width), lambda b, j: (b, nt - 1 - j, 0))
    else:
        def tile(width):
            return pl.BlockSpec((1, tt, width), lambda b, j: (b, j, 0))
    in_specs = [tile(GLA_DK), tile(GLA_DK), tile(GLA_DV), tile(GLA_DK),
                pl.BlockSpec((1, GLA_HEADS, GLA_HEAD_DK, GLA_HEAD_DV), lambda b, j: (b, 0, 0, 0))]
    args = [q, k, v, lg, s0]
    if reverse:
        o_f, og, gn = extra
        in_specs += [tile(GLA_DV), tile(GLA_DV), pl.BlockSpec(gn.shape, lambda b, j: (0, 0))]
        args += [o_f, og, gn]
    return pl.pallas_call(
        functools.partial(_gla_kernel, reverse=reverse),
        grid=(bsz, nt),
        in_specs=in_specs,
        out_specs=tile(GLA_DV),
        out_shape=jax.ShapeDtypeStruct((bsz, t, GLA_DV), BF16 if reverse else F32),
        scratch_shapes=[pltpu.VMEM((GLA_HEADS, GLA_HEAD_DK, GLA_HEAD_DV), F32),
                        pltpu.VMEM((tt, GLA_DK), F32)],
        compiler_params=pltpu.CompilerParams(
            dimension_semantics=("parallel", "arbitrary"), vmem_limit_bytes=VMEM_LIMIT),
        name="gla_bwd" if reverse else "gla_fwd",
    )(*args)


def _split_bf16(x):
    hi = x.astype(BF16)
    return hi, (x - hi.astype(F32)).astype(BF16)


def _dot_split(a_hi, a_lo, b_hi, b_lo):
    return _dot(a_hi, b_hi) + _dot(a_hi, b_lo) + _dot(a_lo, b_hi)


def _filter_mlp_kernel(z_ref, w1_ref, b1_ref, w2_ref, b2_ref, w3_ref, b3_ref, fr_ref, o_ref):
    fr = fr_ref[...]
    h = jnp.sin(fr * (_dot(z_ref[...], w1_ref[...], HI) + b1_ref[...]))
    h = jnp.sin(fr * (_dot(h, w2_ref[...], HI) + b2_ref[...]))
    o_ref[...] = jnp.sin(fr * (_dot(h, w3_ref[...], HI) + b3_ref[...]))


def _filter_mlp_call(zp, w1, b1, w2, b2, w3, b3, fr):
    return pl.pallas_call(
        _filter_mlp_kernel,
        out_shape=jax.ShapeDtypeStruct((zp.shape[0], 2 * FILTER_HIDDEN), F32),
        compiler_params=pltpu.CompilerParams(vmem_limit_bytes=VMEM_LIMIT),
        name="filter_mlp",
    )(zp, w1, b1, w2, b2, w3, b3, fr)


def _filter_kernel(hdn_ref, wfh_ref, wfl_ref, wbh_ref, wbl_ref, t_ref, dl_ref, f2h_ref, f2l_ref,
                   twr_ref, twi_ref, khat_ref, kern_ref, a_ref):
    dl = dl_ref[...]
    blk = 512

    def taps_body(i, ss):
        r0 = pl.multiple_of(i * blk, blk)
        rows = pl.ds(r0, blk)
        hf = _dot_split(*_split_bf16(hdn_ref[0, rows, :]), wfh_ref[...], wfl_ref[...])
        hb = _dot_split(*_split_bf16(hdn_ref[1, rows, :]), wbh_ref[...], wbl_ref[...])
        hf = hf * jnp.exp(-_tile_lanes(t_ref[0, rows, :], CG) * dl)
        hb = hb * jnp.exp(-_tile_lanes(t_ref[1, rows, :], CG) * dl)
        lag = r0 + lax.broadcasted_iota(jnp.int32, hb.shape, 0)
        hb = jnp.where(lag == 0, 0.0, hb)
        kern_ref[rows, :] = hf
        kern_ref[pl.ds(SEQ + r0, blk), :] = hb
        return (ss + jnp.sum(hf * hf, axis=0, keepdims=True)
                + jnp.sum(hb * hb, axis=0, keepdims=True))
    ss = lax.fori_loop(0, SEQ // blk, taps_body, jnp.zeros((1, CG), F32))
    scale = lax.rsqrt(ss + EPS) * (1.0 / FFT_N)

    def load_slab(s, r):
        return kern_ref[pl.ds(s * FFT_N2 + r, ROWS), :]
    _slow_forward(load_slab, FFT_N1, a_ref, CG)

    def body(k1, carry):
        ar = a_ref[k1, 0:FFT_N2, :]
        ai = a_ref[k1, FFT_N2:2 * FFT_N2, :]
        twr = _tile_lanes(twr_ref[k1], CG)
        twi = _tile_lanes(twi_ref[k1], CG)
        a = jnp.concatenate([ar * twr - ai * twi, ar * twi + ai * twr], axis=0)
        x = _dot_split(f2h_ref[...], f2l_ref[...], *_split_bf16(a))
        khat_ref[k1] = x * scale
        return carry
    lax.fori_loop(0, FFT_K1, body, 0)


def _filter_call(hdn, wout_hi, wout_lo, t2, deltas, f2_hi, f2_lo, twr, twi):
    ng = D_HYENA // CG
    def whole(a):
        return pl.BlockSpec(a.shape, lambda g: (0,) * a.ndim)
    fwd_cols = pl.BlockSpec((FILTER_HIDDEN, CG), lambda g: (0, g))
    bwd_cols = pl.BlockSpec((FILTER_HIDDEN, CG), lambda g: (0, ng + g))
    return pl.pallas_call(
        _filter_kernel,
        grid=(ng,),
        in_specs=[whole(hdn), fwd_cols, fwd_cols, bwd_cols, bwd_cols,
                  whole(t2),
                  pl.BlockSpec((1, CG), lambda g: (0, g)),
                  whole(f2_hi), whole(f2_lo), whole(twr), whole(twi)],
        out_specs=pl.BlockSpec((FFT_K1, 2 * FFT_N2, CG), lambda g: (0, 0, g)),
        out_shape=jax.ShapeDtypeStruct((FFT_K1, 2 * FFT_N2, D_HYENA), F32),
        scratch_shapes=[pltpu.VMEM((FFT_N, CG), F32),
                        pltpu.VMEM((FFT_K1, 2 * FFT_N2, CG), F32)],
        compiler_params=pltpu.CompilerParams(vmem_limit_bytes=VMEM_LIMIT),
        name="filter_spec",
    )(hdn, wout_hi, wout_lo, wout_hi, wout_lo, t2, deltas, f2_hi, f2_lo, twr, twi)


def _freq_forward(k1, a_ref, ff_ref, twr_ref, twi_ref):
    ar = a_ref[k1, 0:FFT_N2, :]
    ai = a_ref[k1, FFT_N2:2 * FFT_N2, :]
    twr = _tile_lanes(twr_ref[k1], CG)
    twi = _tile_lanes(twi_ref[k1], CG)
    a = jnp.concatenate([ar * twr - ai * twi, ar * twi + ai * twr], axis=0).astype(BF16)
    return _dot(ff_ref[...], a)


def _freq_inverse(k1, x, a_ref, khat_ref, fi_ref, twr_ref, twi_ref):
    xr, xi = x[0:FFT_N2], x[FFT_N2:]
    kr = khat_ref[k1, 0:FFT_N2, :]
    ki = khat_ref[k1, FFT_N2:2 * FFT_N2, :]
    y = jnp.concatenate([xr * kr - xi * ki, xr * ki + xi * kr], axis=0).astype(BF16)
    bq = _dot(fi_ref[...], y)
    br, bi = bq[0:FFT_N2], bq[FFT_N2:]
    twr = _tile_lanes(twr_ref[k1], CG)
    twi = _tile_lanes(twi_ref[k1], CG)
    a_ref[k1, 0:FFT_N2, :] = br * twr + bi * twi
    a_ref[k1, FFT_N2:2 * FFT_N2, :] = bi * twr - br * twi


def _hyena_kernel(x0_ref, uin_ref, bias_ref, khat_ref, ff_ref, fi_ref, twr_ref, twi_ref,
                  o_ref, u_ref, a_ref):
    blk = FFT_N2

    def load_body(i, carry):
        rows = pl.ds(pl.multiple_of(i * blk, blk), blk)
        u_ref[rows, :] = uin_ref[0, rows, :].astype(F32)
        return carry
    lax.fori_loop(0, SEQ // blk, load_body, 0)

    def load_slab(s, r):
        return u_ref[pl.ds(s * FFT_N2 + r, ROWS), :]
    _slow_forward(load_slab, FFT_N1 // 2, a_ref, CG)

    def freq_pair(ks):
        xs = [_freq_forward(k1, a_ref, ff_ref, twr_ref, twi_ref) for k1 in ks]
        for k1, x in zip(ks, xs):
            _freq_inverse(k1, x, a_ref, khat_ref, fi_ref, twr_ref, twi_ref)

    def freq_body(i, carry):
        freq_pair((2 * i, 2 * i + 1))
        return carry
    lax.fori_loop(0, FFT_K1 // 2, freq_body, 0)
    freq_pair((FFT_K1 - 1,))

    bias = bias_ref[...]

    def inv_body(i, carry):
        r = pl.multiple_of(i * ROWS, ROWS)
        spec = [(a_ref[k1, pl.ds(r, ROWS), :], a_ref[k1, pl.ds(FFT_N2 + r, ROWS), :])
                for k1 in range(FFT_K1)]
        ys = _irfft_sym(spec, FFT_N1, FFT_N1 // 2)
        for s in range(FFT_N1 // 2):
            rows = pl.ds(s * FFT_N2 + r, ROWS)
            u_ref[rows, :] = ys[s] + bias * u_ref[rows, :]
        return carry
    lax.fori_loop(0, FFT_N2 // ROWS, inv_body, 0)

    def out_body(i, carry):
        rows = pl.ds(pl.multiple_of(i * blk, blk), blk)
        o_ref[0, rows, :] = (x0_ref[0, rows, :].astype(F32) * u_ref[rows, :]).astype(o_ref.dtype)
        return carry
    lax.fori_loop(0, SEQ // blk, out_body, 0)


def _hyena_call(x0c, u, bias, khat, ff, fi, twr, twi):
    bsz = u.shape[0]
    ng = D_HYENA // CG
    stream = pl.BlockSpec((1, SEQ, CG), lambda g, b: (b, 0, g))
    def whole(a):
        return pl.BlockSpec(a.shape, lambda g, b: (0,) * a.ndim)
    return pl.pallas_call(
        _hyena_kernel,
        grid=(ng, bsz),
        in_specs=[stream, stream,
                  pl.BlockSpec((1, CG), lambda g, b: (0, g)),
                  pl.BlockSpec((FFT_K1, 2 * FFT_N2, CG), lambda g, b: (0, 0, g)),
                  whole(ff), whole(fi), whole(twr), whole(twi)],
        out_specs=stream,
        out_shape=jax.ShapeDtypeStruct((bsz, SEQ, D_HYENA), BF16),
        scratch_shapes=[pltpu.VMEM((SEQ, CG), F32),
                        pltpu.VMEM((FFT_K1, 2 * FFT_N2, CG), F32)],
        compiler_params=pltpu.CompilerParams(
            dimension_semantics=("parallel", "parallel"), vmem_limit_bytes=VMEM_LIMIT),
        name="hyena",
    )(x0c, u, bias, khat, ff, fi, twr, twi)


def _out_kernel(x_ref, yh_ref, yg_ref, mod_ref, wo_ref, nm_ref, w1_ref, w2_ref, nf_ref, o_ref):
    d = D_MODEL
    g1 = mod_ref[0, :, 2 * d:3 * d]
    sh2 = mod_ref[0, :, 3 * d:4 * d]
    sc2 = mod_ref[0, :, 4 * d:5 * d]
    g2 = mod_ref[0, :, 5 * d:6 * d]
    mix = _dot(yh_ref[0], wo_ref[0:D_HYENA, :]) + _dot(yg_ref[0], wo_ref[D_HYENA:d, :])
    x1 = x_ref[0] + g1 * mix
    h2 = ((_rms(x1) * nm_ref[...]) * (1.0 + sc2) + sh2).astype(BF16)
    acc = jnp.zeros(x1.shape, F32)
    for j in range(D_FF // d):
        hid = _dot(h2, w1_ref[:, j * d:(j + 1) * d])
        hid = jnp.square(jnp.maximum(hid, 0.0)).astype(BF16)
        acc = acc + _dot(hid, w2_ref[j * d:(j + 1) * d, :])
    x2 = x1 + g2 * acc
    o_ref[0] = _rms(x2) * nf_ref[...]


def _out_call(x, yh, yg, mod, wo, nm, w1, w2, nf, tm):
    bsz, t, _ = x.shape
    def tile(width):
        return pl.BlockSpec((1, tm, width), lambda b, i: (b, i, 0))
    def whole(a):
        return pl.BlockSpec(a.shape, lambda b, i: (0,) * a.ndim)
    return pl.pallas_call(
        _out_kernel,
        grid=(bsz, t // tm),
        in_specs=[tile(D_MODEL), tile(D_HYENA), tile(GLA_DV),
                  pl.BlockSpec((1, 1, 6 * D_MODEL), lambda b, i: (b, 0, 0)),
                  whole(wo), whole(nm), whole(w1), whole(w2), whole(nf)],
        out_specs=tile(D_MODEL),
        out_shape=jax.ShapeDtypeStruct((bsz, t, D_MODEL), F32),
        compiler_params=pltpu.CompilerParams(
            dimension_semantics=("parallel", "parallel"), vmem_limit_bytes=VMEM_LIMIT),
        name="out_mlp",
    )(x, yh, yg, mod, wo, nm, w1, w2, nf)


def _lag_reverse(a):
    return jnp.concatenate([a[0:1], a[:0:-1]], axis=0)


def _pair(w):
    z = jnp.zeros_like(w)
    return jnp.block([[w, z], [z, w]])


def _filter_features():
    length = SEQ
    t = jnp.linspace(0.0, 1.0, length, dtype=F32)[:, None]
    w = 2.0 * math.pi * jnp.arange(length, dtype=F32) / length
    f = jnp.linspace(1e-4, FILTER_BANDS - 1, FILTER_BANDS, dtype=F32)
    ang = w[:, None] * f[None, :]
    z = jnp.concatenate([t, jnp.cos(ang), -jnp.sin(ang)], axis=-1)
    z = jnp.pad(z, ((0, 0), (0, 128 - FILTER_EMB)))
    zp = jnp.concatenate([z[:length // 2], z[length // 2:]], axis=1)
    t2 = jnp.broadcast_to(jnp.stack([t, _lag_reverse(t)]), (2, length, 128))
    min_decay = math.log(DECAY_TARGET) / SLOW_DECAY_PCT
    max_decay = math.log(DECAY_TARGET) / FAST_DECAY_PCT
    deltas = jnp.abs(jnp.linspace(min_decay, max_decay, D_HYENA, dtype=F32))[None, :]
    return zp, t2, deltas


def _dft_tables():
    n2 = jnp.arange(FFT_N2, dtype=jnp.int32)
    k1 = jnp.arange(FFT_K1, dtype=jnp.int32)
    ang_tw = (2.0 * math.pi / FFT_N) * (k1[:, None] * n2[None, :]).astype(F32)
    twr = jnp.broadcast_to(jnp.cos(ang_tw)[:, :, None], (FFT_K1, FFT_N2, 128))
    twi = jnp.broadcast_to(-jnp.sin(ang_tw)[:, :, None], (FFT_K1, FFT_N2, 128))
    ang = (2.0 * math.pi / FFT_N2) * ((n2[:, None] * n2[None, :]) % FFT_N2).astype(F32)
    cr, ci = jnp.cos(ang), -jnp.sin(ang)
    fwd = jnp.block([[cr, -ci], [ci, cr]])
    inv = jnp.block([[cr, ci], [-ci, cr]])
    return fwd, inv, twr, twi


def kernel(x, c, ctx, c_ctx, w_ada, b_ada, norm_mix, norm_mlp, w_in, conv_w, conv_b, filt_w1, filt_b1, filt_w2, filt_b2, filt_w3, filt_b3, filt_freq, filt_wout, hyena_bias, gk_w_fwd, gk_b_fwd, gk_w_bwd, gk_b_bwd, gla_norm, w_out, w_mlp1, w_mlp2, norm_final):
    bsz = x.shape[0]
    l = 0
    cc = jnp.concatenate([c, c_ctx[None, :], jnp.zeros((16 - bsz - 1, D_MODEL), F32)], axis=0)
    mod_all = _mod_call(cc, w_ada[l], b_ada[l][None, :])
    mod = mod_all[:bsz][:, None, :]
    mod_c = jnp.broadcast_to(mod_all[bsz][None, None, :], (bsz, 1, 6 * D_MODEL))

    w_in_p = jnp.pad(w_in[l], ((0, 0), (0, N_IN_PAD - N_IN))).astype(BF16)
    gw = jnp.zeros((N_IN_PAD - GATE_COL, 2 * GLA_DK), F32)
    gw = gw.at[0:GATE_RANK, 0:GLA_DK].set(gk_w_fwd[l])
    gw = gw.at[GATE_RANK:2 * GATE_RANK, GLA_DK:].set(gk_w_bwd[l])
    gb = jnp.concatenate([gk_b_fwd[l], gk_b_bwd[l]])[None, :]
    gw = gw.astype(BF16)
    g_mix = norm_mix[l][None, :]

    conv = (conv_w[l], conv_b[l][None, :])
    x0c, u, q, k, v, og, lgf, lgb = _in_proj_call(x, mod, g_mix, w_in_p, gw, gb, conv, 512)
    k_c, v_c, lgf_c, lgb_c = _in_proj_call(ctx, mod_c, g_mix, w_in_p, gw, gb, None, ctx.shape[1])

    s_f, s_b = _ctx_state_call(k_c, v_c, lgf_c, lgb_c)
    o_f = _gla_call(q, k, v, lgf, s_f, None, False, 512)
    y_gla = _gla_call(q, k, v, lgb, s_b, (o_f, og, gla_norm[l][None, :]), True, 512)

    zp, t2, deltas = _filter_features()
    f2_fwd, f2_inv, twr, twi = _dft_tables()
    def two(b):
        return jnp.concatenate([b, b])[None, :]
    w1p = jnp.pad(filt_w1[l], ((0, 128 - FILTER_EMB), (0, 0)))
    hp = _filter_mlp_call(zp, _pair(w1p), two(filt_b1[l]), _pair(filt_w2[l]), two(filt_b2[l]),
                          _pair(filt_w3[l]), two(filt_b3[l]), two(filt_freq[l]))
    hdn = jnp.concatenate([hp[:, :FILTER_HIDDEN], hp[:, FILTER_HIDDEN:]], axis=0)
    hdn2 = jnp.stack([hdn, _lag_reverse(hdn)])
    khat = _filter_call(hdn2, *_split_bf16(filt_wout[l]), t2, deltas, *_split_bf16(f2_fwd), twr, twi)
    y_hy = _hyena_call(x0c, u, hyena_bias[l][None, :], khat,
                       f2_fwd.astype(BF16), f2_inv.astype(BF16), twr, twi)

    return _out_call(x, y_hy, y_gla, mod, w_out[l].astype(BF16), norm_mlp[l][None, :],
                     w_mlp1[l].astype(BF16), w_mlp2[l].astype(BF16), norm_final[None, :], 512)
```

```python
import functools
import math

import jax
import jax.numpy as jnp
from jax import lax
from jax.experimental import pallas as pl
from jax.experimental.pallas import tpu as pltpu

D_MODEL = 1024
SEQ = 4096
GRID_W = 64
D_HYENA = 512
GLA_HEADS = 4
GLA_DK = 256
GLA_DV = 512
GLA_HEAD_DK = 64
GLA_HEAD_DV = 128
GATE_RANK = 16
GATE_NORMALIZER = 16.0
GLA_CHUNK = 64
FILTER_EMB = 33
FILTER_BANDS = 16
FILTER_HIDDEN = 64
FAST_DECAY_PCT = 0.3
SLOW_DECAY_PCT = 1.5
DECAY_TARGET = 1e-2
D_FF = 4 * D_MODEL
EPS = 1e-6
N_IN = 3104
N_IN_PAD = 3200
GATE_COL = 3072

FFT_N = 2 * SEQ
FFT_N1 = 32
FFT_N2 = 256
FFT_K1 = FFT_N1 // 2 + 1
CG = 256
ROWS = 16

F32 = jnp.float32
BF16 = jnp.bfloat16
HI = lax.Precision.HIGHEST
VMEM_LIMIT = 56 * 1024 * 1024


def _dot(a, b, precision=None):
    return jnp.dot(a, b, preferred_element_type=F32, precision=precision)


def _add(a, b):
    if a is None:
        return b
    if b is None:
        return a
    return a + b


def _sub(a, b):
    if b is None:
        return a
    if a is None:
        return -b
    return a - b


def _scale(a, c):
    if a is None or c == 0.0:
        return None
    if c == 1.0:
        return a
    if c == -1.0:
        return -a
    return a * c


def _cadd(x, y):
    return (_add(x[0], y[0]), _add(x[1], y[1]))


def _csub(x, y):
    return (_sub(x[0], y[0]), _sub(x[1], y[1]))


def _conj(x):
    return (x[0], None if x[1] is None else -x[1])


def _snap(v):
    for t in (0.0, 1.0, -1.0):
        if abs(v - t) < 1e-12:
            return t
    return v


def _cmul_const(x, ang):
    c, s = _snap(math.cos(ang)), _snap(math.sin(ang))
    re = _sub(_scale(x[0], c), _scale(x[1], s))
    im = _add(_scale(x[0], s), _scale(x[1], c))
    return (re, im)


def _rfft_sym(x):
    n = len(x)
    if n == 1:
        return [(x[0], None)]
    if n == 2:
        return [(_add(x[0], x[1]), None), (_sub(x[0], x[1]), None)]
    ev = _rfft_sym(x[0::2])
    od = _rfft_sym(x[1::2])
    out = [None] * (n // 2 + 1)
    for k in range(n // 4 + 1):
        t = _cmul_const(od[k], -2.0 * math.pi * k / n)
        out[k] = _cadd(ev[k], t)
        if n // 2 - k != k:
            out[n // 2 - k] = _conj(_csub(ev[k], t))
    out[0] = (out[0][0], None)
    out[n // 2] = (out[n // 2][0], None)
    return out


def _irfft_sym(b, n, keep):
    if keep <= 0:
        return [None] * n
    if n == 1:
        return [b[0][0]]
    if n == 2:
        y1 = _sub(b[0][0], b[1][0]) if keep > 1 else None
        return [_add(b[0][0], b[1][0]), y1]
    b = list(b)
    b[0] = (b[0][0], None)
    b[n // 2] = (b[n // 2][0], None)
    ev, od = [], []
    for k in range(n // 4 + 1):
        cb = _conj(b[n // 2 - k])
        ev.append(_cadd(b[k], cb))
        od.append(_cmul_const(_csub(b[k], cb), 2.0 * math.pi * k / n))
    y = [None] * n
    y[0::2] = _irfft_sym(ev, n // 2, (keep + 1) // 2)
    y[1::2] = _irfft_sym(od, n // 2, keep // 2)
    return y


def _slow_forward(load_slab, n_slabs, a_ref, width):
    def body(i, carry):
        r = pl.multiple_of(i * ROWS, ROWS)
        xs = [load_slab(s, r) for s in range(n_slabs)] + [None] * (FFT_N1 - n_slabs)
        spec = _rfft_sym(xs)
        zero = jnp.zeros((ROWS, width), F32)
        for k1 in range(FFT_K1):
            re, im = spec[k1]
            a_ref[k1, pl.ds(r, ROWS), :] = zero if re is None else re
            a_ref[k1, pl.ds(FFT_N2 + r, ROWS), :] = zero if im is None else im
        return carry
    lax.fori_loop(0, FFT_N2 // ROWS, body, 0)


def _tile_lanes(t, width):
    reps = width // t.shape[-1]
    return t if reps == 1 else jnp.concatenate([t] * reps, axis=-1)


def _mod_kernel(c_ref, w_ref, b_ref, o_ref):
    c = c_ref[...]
    s = c * jax.nn.sigmoid(c)
    o_ref[...] = _dot(s, w_ref[...], HI) + b_ref[...]


def _mod_call(cc, w, b):
    n = w.shape[1]
    bn = 1024
    return pl.pallas_call(
        _mod_kernel,
        grid=(n // bn,),
        in_specs=[pl.BlockSpec(cc.shape, lambda j: (0, 0)),
                  pl.BlockSpec((D_MODEL, bn), lambda j: (0, j)),
                  pl.BlockSpec((1, bn), lambda j: (0, j))],
        out_specs=pl.BlockSpec((cc.shape[0], bn), lambda j: (0, j)),
        out_shape=jax.ShapeDtypeStruct((cc.shape[0], n), F32),
        compiler_params=pltpu.CompilerParams(vmem_limit_bytes=VMEM_LIMIT),
        name="mod",
    )(cc, w, b)


def _rms(x):
    return x * lax.rsqrt(jnp.mean(x * x, axis=-1, keepdims=True) + EPS)


def _short_conv(x, w, b):
    n = x.shape[0]
    pos = lax.broadcasted_iota(jnp.int32, x.shape, 0) & (GRID_W - 1)
    prev = jnp.where(pos == 0, 0.0, pltpu.roll(x, 1, 0))
    nxt = jnp.where(pos == GRID_W - 1, 0.0, pltpu.roll(x, n - 1, 0))
    return b + w[0:1, :] * prev + w[1:2, :] * x + w[2:3, :] * nxt


def _in_proj_kernel(*refs, latent):
    if latent:
        (x_ref, mod_ref, g_ref, w_ref, gw_ref, gb_ref, cw_ref, cb_ref,
         x0_ref, u_ref, q_ref, k_ref, v_ref, og_ref, lgf_ref, lgb_ref) = refs
    else:
        x_ref, mod_ref, g_ref, w_ref, gw_ref, gb_ref, k_ref, v_ref, lgf_ref, lgb_ref = refs
    x = x_ref[0]
    sh = mod_ref[0, :, 0:D_MODEL]
    sc = mod_ref[0, :, D_MODEL:2 * D_MODEL]
    h = ((_rms(x) * g_ref[...]) * (1.0 + sc) + sh).astype(BF16)
    c0 = 3 * D_HYENA
    c1 = c0 + GLA_DK
    c2 = c1 + GLA_DK
    c3 = c2 + GLA_DV
    gr = _dot(h, w_ref[:, GATE_COL:N_IN_PAD]).astype(BF16)
    z = _dot(gr, gw_ref[...]) + gb_ref[...]
    lg = jax.nn.log_sigmoid(z) * (1.0 / GATE_NORMALIZER)
    lgf_ref[0] = lg[:, 0:GLA_DK]
    lgb_ref[0] = lg[:, GLA_DK:2 * GLA_DK]
    if latent:
        dh = D_HYENA
        conv = [_short_conv(_dot(h, w_ref[:, j * dh:(j + 1) * dh]),
                            cw_ref[:, j * dh:(j + 1) * dh], cb_ref[:, j * dh:(j + 1) * dh])
                for j in range(3)]
        x0_ref[0] = conv[0].astype(BF16)
        u_ref[0] = (conv[1] * conv[2]).astype(BF16)
        q = _dot(h, w_ref[:, c0:c1]) * (GLA_HEAD_DK ** -0.5)
        q_ref[0] = q.astype(BF16)
        og_ref[0] = _dot(h, w_ref[:, c3:c3 + GLA_DV]).astype(BF16)
    k_ref[0] = _dot(h, w_ref[:, c1:c2]).astype(BF16)
    v_ref[0] = _dot(h, w_ref[:, c2:c3]).astype(BF16)


def _in_proj_call(x, mod, g, w, gw, gb, conv, tm):
    bsz, t, _ = x.shape
    latent = conv is not None
    def tile(width):
        return pl.BlockSpec((1, tm, width), lambda b, i: (b, i, 0))
    def whole(a):
        return pl.BlockSpec(a.shape, lambda b, i: (0,) * a.ndim)
    if latent:
        widths = (D_HYENA, D_HYENA, GLA_DK, GLA_DK, GLA_DV, GLA_DV, GLA_DK, GLA_DK)
        dtypes = (BF16,) * 6 + (F32, F32)
    else:
        widths = (GLA_DK, GLA_DV, GLA_DK, GLA_DK)
        dtypes = (BF16, BF16, F32, F32)
    args = [x, mod, g, w, gw, gb] + (list(conv) if latent else [])
    return pl.pallas_call(
        functools.partial(_in_proj_kernel, latent=latent),
        grid=(bsz, t // tm),
        in_specs=[tile(D_MODEL),
                  pl.BlockSpec((1, 1, 6 * D_MODEL), lambda b, i: (b, 0, 0))]
                 + [whole(a) for a in args[2:]],
        out_specs=[tile(wd) for wd in widths],
        out_shape=[jax.ShapeDtypeStruct((bsz, t, wd), dt) for wd, dt in zip(widths, dtypes)],
        compiler_params=pltpu.CompilerParams(
            dimension_semantics=("parallel", "parallel"), vmem_limit_bytes=VMEM_LIMIT),
        name="in_proj" if latent else "in_proj_ctx",
    )(*args)


def _ctx_state_kernel(k_ref, v_ref, lgf_ref, lgb_ref, sf_ref, sb_ref):
    t = k_ref.shape[1]
    row = lax.broadcasted_iota(jnp.int32, (t, t), 0)
    col = lax.broadcasted_iota(jnp.int32, (t, t), 1)
    k = k_ref[0].astype(F32)
    v = v_ref[0]
    for lg_ref, tri, o_ref in ((lgf_ref, col > row, sf_ref), (lgb_ref, col < row, sb_ref)):
        rem = _dot(tri.astype(F32), lg_ref[0], HI)
        kend = (k * jnp.exp(rem)).astype(BF16)
        ds = lax.dot_general(kend, v, (((0,), (0,)), ((), ())), preferred_element_type=F32)
        for h in range(GLA_HEADS):
            o_ref[0, h] = ds[h * GLA_HEAD_DK:(h + 1) * GLA_HEAD_DK,
                             h * GLA_HEAD_DV:(h + 1) * GLA_HEAD_DV]


def _ctx_state_call(k, v, lgf, lgb):
    bsz, t, _ = k.shape
    def tile(width):
        return pl.BlockSpec((1, t, width), lambda b: (b, 0, 0))
    st = pl.BlockSpec((1, GLA_HEADS, GLA_HEAD_DK, GLA_HEAD_DV), lambda b: (b, 0, 0, 0))
    return pl.pallas_call(
        _ctx_state_kernel,
        grid=(bsz,),
        in_specs=[tile(GLA_DK), tile(GLA_DV), tile(GLA_DK), tile(GLA_DK)],
        out_specs=[st, st],
        out_shape=[jax.ShapeDtypeStruct((bsz, GLA_HEADS, GLA_HEAD_DK, GLA_HEAD_DV), F32)] * 2,
        compiler_params=pltpu.CompilerParams(vmem_limit_bytes=VMEM_LIMIT),
        name="gla_ctx",
    )(k, v, lgf, lgb)


def _prefix_rows(x, reverse):
    n = x.shape[0]
    pos = lax.broadcasted_iota(jnp.int32, x.shape, 0) & (GLA_CHUNK - 1)
    k = 1
    while k < GLA_CHUNK:
        if reverse:
            x = x + jnp.where(pos < GLA_CHUNK - k, pltpu.roll(x, n - k, 0), 0.0)
        else:
            x = x + jnp.where(pos >= k, pltpu.roll(x, k, 0), 0.0)
        k *= 2
    return x


def _pair_diag(a, b):
    z = jnp.zeros_like(a)
    return jnp.concatenate([jnp.concatenate([a, z], axis=1),
                            jnp.concatenate([z, b], axis=1)], axis=0)


def _gla_stage1(q, k, v, b, reverse):
    c, hd, hv = GLA_CHUNK, GLA_HEAD_DK, GLA_HEAD_DV
    total = b[0:1] if reverse else b[c - 1:c]
    qin = (q * jnp.exp(b)).astype(BF16)
    kin = k * jnp.exp(-b)
    kend = k * jnp.exp(total - b)
    dec = jnp.exp(total)
    hi = dec.astype(BF16).astype(F32)
    mid = (dec - hi).astype(BF16).astype(F32)
    lo = dec - hi - mid
    r8 = lax.broadcasted_iota(jnp.int32, (8, GLA_DK), 0)
    extra = jnp.where(r8 == 0, hi, jnp.where(r8 == 1, mid, jnp.where(r8 == 2, lo, 0.0)))
    a = jnp.concatenate([kend, extra, jnp.zeros((c - 8, GLA_DK), F32)], axis=0)
    at = a.T.astype(BF16)
    r64 = lax.broadcasted_iota(jnp.int32, (c, 2 * hv), 0)
    l64 = lax.broadcasted_iota(jnp.int32, (c, 2 * hv), 1)
    ones_blk = jnp.where((r64 < 3) & (l64 >= hv), 1.0, 0.0).astype(BF16)
    zv = jnp.zeros((c, hv), BF16)
    key = lax.broadcasted_iota(jnp.int32, (c, 2 * hd), 1)
    qrow = lax.broadcasted_iota(jnp.int32, (c, 2 * hd), 0)
    att_mask = ((key & (hd - 1)) >= qrow) if reverse else ((key & (hd - 1)) <= qrow)
    atts = []
    for p in range(GLA_HEADS // 2):
        lanes = slice(p * 2 * hd, (p + 1) * 2 * hd)
        k2 = kin[:, lanes]
        kbd = jnp.concatenate([jnp.where(key < hd, k2, 0.0), jnp.where(key >= hd, k2, 0.0)],
                              axis=0).astype(BF16)
        att = lax.dot_general(qin[:, lanes], kbd, (((1,), (1,)), ((), ())),
                              preferred_element_type=F32)
        atts.append(jnp.where(att_mask, att, 0.0).astype(BF16))
    incs = []
    for h in range(GLA_HEADS):
        vh = v[:, h * hv:(h + 1) * hv]
        incs.append(_dot(at[h * hd:(h + 1) * hd, :],
                         jnp.concatenate([jnp.concatenate([vh, zv], axis=1), ones_blk], axis=0)))
    return qin, atts, incs


def _gla_stage2(qin, atts, incs, v, s):
    hd, hv = GLA_HEAD_DK, GLA_HEAD_DV
    outs = []
    for p in range(GLA_HEADS // 2):
        h0, h1 = 2 * p, 2 * p + 1
        rhs = jnp.concatenate([_pair_diag(v[:, h0 * hv:(h0 + 1) * hv], v[:, h1 * hv:(h1 + 1) * hv]),
                               _pair_diag(s[h0].astype(BF16), s[h1].astype(BF16))], axis=0)
        lhs = jnp.concatenate([atts[p], qin[:, p * 2 * hd:(p + 1) * 2 * hd]], axis=1)
        outs.append(_dot(lhs, rhs))
    s_new = [inc[:, hv:] * sh + inc[:, :hv] for inc, sh in zip(incs, s)]
    return jnp.concatenate(outs, axis=1), s_new


def _gla_kernel(*refs, reverse):
    if reverse:
        q_ref, k_ref, v_ref, lg_ref, s0_ref, of_ref, og_ref, gn_ref, o_ref, s_scr, b_scr = refs
    else:
        q_ref, k_ref, v_ref, lg_ref, s0_ref, o_ref, s_scr, b_scr = refs
    c = GLA_CHUNK

    @pl.when(pl.program_id(1) == 0)
    def _():
        s_scr[...] = s0_ref[0]

    b_scr[...] = _prefix_rows(lg_ref[0], reverse)
    n_chunks = q_ref.shape[1] // c
    order = list(range(n_chunks - 1, -1, -1) if reverse else range(n_chunks))

    def start(ci):
        rows = slice(ci * c, (ci + 1) * c)
        return _gla_stage1(q_ref[0, rows, :].astype(F32), k_ref[0, rows, :].astype(F32),
                           v_ref[0, rows, :], b_scr[rows, :], reverse)

    def finish(ci, staged, s):
        rows = slice(ci * c, (ci + 1) * c)
        o, s = _gla_stage2(*staged, v_ref[0, rows, :], s)
        if reverse:
            o = o + of_ref[0, rows, :]
            og = og_ref[0, rows, :].astype(F32)
            parts = []
            for h in range(GLA_HEADS):
                oh = o[:, h * GLA_HEAD_DV:(h + 1) * GLA_HEAD_DV]
                parts.append(_rms(oh) * gn_ref[...])
            y = jnp.concatenate(parts, axis=1) * (og * jax.nn.sigmoid(og))
            o_ref[0, rows, :] = y.astype(o_ref.dtype)
        else:
            o_ref[0, rows, :] = o
        return s

    s = [s_scr[h] for h in range(GLA_HEADS)]
    staged = start(order[0])
    for prev, ci in zip(order, order[1:]):
        nxt = start(ci)
        s = finish(prev, staged, s)
        staged = nxt
    s = finish(order[-1], staged, s)
    for h in range(GLA_HEADS):
        s_scr[h] = s[h]


def _gla_call(q, k, v, lg, s0, extra, reverse, tt):
    bsz, t, _ = q.shape
    nt = t // tt
    if reverse:
        def tile(width):
            return pl.BlockSpec((1, tt, width), lambda b, j: (b, nt - 1 - j, 0))
    else:
        def tile(width):
            return pl.BlockSpec((1, tt, width), lambda b, j: (b, j, 0))
    in_specs = [tile(GLA_DK), tile(GLA_DK), tile(GLA_DV), tile(GLA_DK),
                pl.BlockSpec((1, GLA_HEADS, GLA_HEAD_DK, GLA_HEAD_DV), lambda b, j: (b, 0, 0, 0))]
    args = [q, k, v, lg, s0]
    if reverse:
        o_f, og, gn = extra
        in_specs += [tile(GLA_DV), tile(GLA_DV), pl.BlockSpec(gn.shape, lambda b, j: (0, 0))]
        args += [o_f, og, gn]
    return pl.pallas_call(
        functools.partial(_gla_kernel, reverse=reverse),
        grid=(bsz, nt),
        in_specs=in_specs,
        out_specs=tile(GLA_DV),
        out_shape=jax.ShapeDtypeStruct((bsz, t, GLA_DV), BF16 if reverse else F32),
        scratch_shapes=[pltpu.VMEM((GLA_HEADS, GLA_HEAD_DK, GLA_HEAD_DV), F32),
                        pltpu.VMEM((tt, GLA_DK), F32)],
        compiler_params=pltpu.CompilerParams(
            dimension_semantics=("parallel", "arbitrary"), vmem_limit_bytes=VMEM_LIMIT),
        name="gla_bwd" if reverse else "gla_fwd",
    )(*args)


def _split_bf16(x):
    hi = x.astype(BF16)
    return hi, (x - hi.astype(F32)).astype(BF16)


def _dot_split(a_hi, a_lo, b_hi, b_lo):
    return _dot(a_hi, b_hi) + _dot(a_hi, b_lo) + _dot(a_lo, b_hi)


def _filter_mlp_kernel(z_ref, w1_ref, b1_ref, w2_ref, b2_ref, w3_ref, b3_ref, fr_ref, o_ref):
    fr = fr_ref[...]
    h = jnp.sin(fr * (_dot(z_ref[...], w1_ref[...], HI) + b1_ref[...]))
    h = jnp.sin(fr * (_dot(h, w2_ref[...], HI) + b2_ref[...]))
    o_ref[...] = jnp.sin(fr * (_dot(h, w3_ref[...], HI) + b3_ref[...]))


def _filter_mlp_call(zp, w1, b1, w2, b2, w3, b3, fr):
    return pl.pallas_call(
        _filter_mlp_kernel,
        out_shape=jax.ShapeDtypeStruct((zp.shape[0], 2 * FILTER_HIDDEN), F32),
        compiler_params=pltpu.CompilerParams(vmem_limit_bytes=VMEM_LIMIT),
        name="filter_mlp",
    )(zp, w1, b1, w2, b2, w3, b3, fr)


def _filter_kernel(hdn_ref, wfh_ref, wfl_ref, wbh_ref, wbl_ref, t_ref, dl_ref, f2h_ref, f2l_ref,
                   twr_ref, twi_ref, khat_ref, kern_ref, a_ref):
    dl = dl_ref[...]
    blk = 512

    def taps_body(i, ss):
        r0 = pl.multiple_of(i * blk, blk)
        rows = pl.ds(r0, blk)
        hf = _dot_split(*_split_bf16(hdn_ref[0, rows, :]), wfh_ref[...], wfl_ref[...])
        hb = _dot_split(*_split_bf16(hdn_ref[1, rows, :]), wbh_ref[...], wbl_ref[...])
        hf = hf * jnp.exp(-_tile_lanes(t_ref[0, rows, :], CG) * dl)
        hb = hb * jnp.exp(-_tile_lanes(t_ref[1, rows, :], CG) * dl)
        lag = r0 + lax.broadcasted_iota(jnp.int32, hb.shape, 0)
        hb = jnp.where(lag == 0, 0.0, hb)
        kern_ref[rows, :] = hf
        kern_ref[pl.ds(SEQ + r0, blk), :] = hb
        return (ss + jnp.sum(hf * hf, axis=0, keepdims=True)
                + jnp.sum(hb * hb, axis=0, keepdims=True))
    ss = lax.fori_loop(0, SEQ // blk, taps_body, jnp.zeros((1, CG), F32))
    scale = lax.rsqrt(ss + EPS) * (1.0 / FFT_N)

    def load_slab(s, r):
        return kern_ref[pl.ds(s * FFT_N2 + r, ROWS), :]
    _slow_forward(load_slab, FFT_N1, a_ref, CG)

    def body(k1, carry):
        ar = a_ref[k1, 0:FFT_N2, :]
        ai = a_ref[k1, FFT_N2:2 * FFT_N2, :]
        twr = _tile_lanes(twr_ref[k1], CG)
        twi = _tile_lanes(twi_ref[k1], CG)
        a = jnp.concatenate([ar * twr - ai * twi, ar * twi + ai * twr], axis=0)
        x = _dot_split(f2h_ref[...], f2l_ref[...], *_split_bf16(a))
        khat_ref[k1] = x * scale
        return carry
    lax.fori_loop(0, FFT_K1, body, 0)


def _filter_call(hdn, wout_hi, wout_lo, t2, deltas, f2_hi, f2_lo, twr, twi):
    ng = D_HYENA // CG
    def whole(a):
        return pl.BlockSpec(a.shape, lambda g: (0,) * a.ndim)
    fwd_cols = pl.BlockSpec((FILTER_HIDDEN, CG), lambda g: (0, g))
    bwd_cols = pl.BlockSpec((FILTER_HIDDEN, CG), lambda g: (0, ng + g))
    return pl.pallas_call(
        _filter_kernel,
        grid=(ng,),
        in_specs=[whole(hdn), fwd_cols, fwd_cols, bwd_cols, bwd_cols,
                  whole(t2),
                  pl.BlockSpec((1, CG), lambda g: (0, g)),
                  whole(f2_hi), whole(f2_lo), whole(twr), whole(twi)],
        out_specs=pl.BlockSpec((FFT_K1, 2 * FFT_N2, CG), lambda g: (0, 0, g)),
        out_shape=jax.ShapeDtypeStruct((FFT_K1, 2 * FFT_N2, D_HYENA), F32),
        scratch_shapes=[pltpu.VMEM((FFT_N, CG), F32),
                        pltpu.VMEM((FFT_K1, 2 * FFT_N2, CG), F32)],
        compiler_params=pltpu.CompilerParams(vmem_limit_bytes=VMEM_LIMIT),
        name="filter_spec",
    )(hdn, wout_hi, wout_lo, wout_hi, wout_lo, t2, deltas, f2_hi, f2_lo, twr, twi)


def _freq_forward(k1, a_ref, ff_ref, twr_ref, twi_ref):
    ar = a_ref[k1, 0:FFT_N2, :]
    ai = a_ref[k1, FFT_N2:2 * FFT_N2, :]
    twr = _tile_lanes(twr_ref[k1], CG)
    twi = _tile_lanes(twi_ref[k1], CG)
    a = jnp.concatenate([ar * twr - ai * twi, ar * twi + ai * twr], axis=0).astype(BF16)
    return _dot(ff_ref[...], a)


def _freq_inverse(k1, x, a_ref, khat_ref, fi_ref, twr_ref, twi_ref):
    xr, xi = x[0:FFT_N2], x[FFT_N2:]
    kr = khat_ref[k1, 0:FFT_N2, :]
    ki = khat_ref[k1, FFT_N2:2 * FFT_N2, :]
    y = jnp.concatenate([xr * kr - xi * ki, xr * ki + xi * kr], axis=0).astype(BF16)
    bq = _dot(fi_ref[...], y)
    br, bi = bq[0:FFT_N2], bq[FFT_N2:]
    twr = _tile_lanes(twr_ref[k1], CG)
    twi = _tile_lanes(twi_ref[k1], CG)
    a_ref[k1, 0:FFT_N2, :] = br * twr + bi * twi
    a_ref[k1, FFT_N2:2 * FFT_N2, :] = bi * twr - br * twi


def _hyena_kernel(x0_ref, uin_ref, bias_ref, khat_ref, ff_ref, fi_ref, twr_ref, twi_ref,
                  o_ref, u_ref, a_ref):
    blk = FFT_N2

    def load_body(i, carry):
        rows = pl.ds(pl.multiple_of(i * blk, blk), blk)
        u_ref[rows, :] = uin_ref[0, rows, :].astype(F32)
        return carry
    lax.fori_loop(0, SEQ // blk, load_body, 0)

    def load_slab(s, r):
        return u_ref[pl.ds(s * FFT_N2 + r, ROWS), :]
    _slow_forward(load_slab, FFT_N1 // 2, a_ref, CG)

    def freq_pair(ks):
        xs = [_freq_forward(k1, a_ref, ff_ref, twr_ref, twi_ref) for k1 in ks]
        for k1, x in zip(ks, xs):
            _freq_inverse(k1, x, a_ref, khat_ref, fi_ref, twr_ref, twi_ref)

    def freq_body(i, carry):
        freq_pair((2 * i, 2 * i + 1))
        return carry
    lax.fori_loop(0, FFT_K1 // 2, freq_body, 0)
    freq_pair((FFT_K1 - 1,))

    bias = bias_ref[...]

    def inv_body(i, carry):
        r = pl.multiple_of(i * ROWS, ROWS)
        spec = [(a_ref[k1, pl.ds(r, ROWS), :], a_ref[k1, pl.ds(FFT_N2 + r, ROWS), :])
                for k1 in range(FFT_K1)]
        ys = _irfft_sym(spec, FFT_N1, FFT_N1 // 2)
        for s in range(FFT_N1 // 2):
            rows = pl.ds(s * FFT_N2 + r, ROWS)
            u_ref[rows, :] = ys[s] + bias * u_ref[rows, :]
        return carry
    lax.fori_loop(0, FFT_N2 // ROWS, inv_body, 0)

    def out_body(i, carry):
        rows = pl.ds(pl.multiple_of(i * blk, blk), blk)
        o_ref[0, rows, :] = (x0_ref[0, rows, :].astype(F32) * u_ref[rows, :]).astype(o_ref.dtype)
        return carry
    lax.fori_loop(0, SEQ // blk, out_body, 0)


def _hyena_call(x0c, u, bias, khat, ff, fi, twr, twi):
    bsz = u.shape[0]
    ng = D_HYENA // CG
    stream = pl.BlockSpec((1, SEQ, CG), lambda g, b: (b, 0, g))
    def whole(a):
        return pl.BlockSpec(a.shape, lambda g, b: (0,) * a.ndim)
    return pl.pallas_call(
        _hyena_kernel,
        grid=(ng, bsz),
        in_specs=[stream, stream,
                  pl.BlockSpec((1, CG), lambda g, b: (0, g)),
                  pl.BlockSpec((FFT_K1, 2 * FFT_N2, CG), lambda g, b: (0, 0, g)),
                  whole(ff), whole(fi), whole(twr), whole(twi)],
        out_specs=stream,
        out_shape=jax.ShapeDtypeStruct((bsz, SEQ, D_HYENA), BF16),
        scratch_shapes=[pltpu.VMEM((SEQ, CG), F32),
                        pltpu.VMEM((FFT_K1, 2 * FFT_N2, CG), F32)],
        compiler_params=pltpu.CompilerParams(
            dimension_semantics=("parallel", "parallel"), vmem_limit_bytes=VMEM_LIMIT),
        name="hyena",
    )(x0c, u, bias, khat, ff, fi, twr, twi)


def _out_kernel(x_ref, yh_ref, yg_ref, mod_ref, wo_ref, nm_ref, w1_ref, w2_ref, nf_ref, o_ref):
    d = D_MODEL
    g1 = mod_ref[0, :, 2 * d:3 * d]
    sh2 = mod_ref[0, :, 3 * d:4 * d]
    sc2 = mod_ref[0, :, 4 * d:5 * d]
    g2 = mod_ref[0, :, 5 * d:6 * d]
    mix = _dot(yh_ref[0], wo_ref[0:D_HYENA, :]) + _dot(yg_ref[0], wo_ref[D_HYENA:d, :])
    x1 = x_ref[0] + g1 * mix
    h2 = ((_rms(x1) * nm_ref[...]) * (1.0 + sc2) + sh2).astype(BF16)
    acc = jnp.zeros(x1.shape, F32)
    for j in range(D_FF // d):
        hid = _dot(h2, w1_ref[:, j * d:(j + 1) * d])
        hid = jnp.square(jnp.maximum(hid, 0.0)).astype(BF16)
        acc = acc + _dot(hid, w2_ref[j * d:(j + 1) * d, :])
    x2 = x1 + g2 * acc
    o_ref[0] = _rms(x2) * nf_ref[...]


def _out_call(x, yh, yg, mod, wo, nm, w1, w2, nf, tm):
    bsz, t, _ = x.shape
    def tile(width):
        return pl.BlockSpec((1, tm, width), lambda b, i: (b, i, 0))
    def whole(a):
        return pl.BlockSpec(a.shape, lambda b, i: (0,) * a.ndim)
    return pl.pallas_call(
        _out_kernel,
        grid=(bsz, t // tm),
        in_specs=[tile(D_MODEL), tile(D_HYENA), tile(GLA_DV),
                  pl.BlockSpec((1, 1, 6 * D_MODEL), lambda b, i: (b, 0, 0)),
                  whole(wo), whole(nm), whole(w1), whole(w2), whole(nf)],
        out_specs=tile(D_MODEL),
        out_shape=jax.ShapeDtypeStruct((bsz, t, D_MODEL), F32),
        compiler_params=pltpu.CompilerParams(
            dimension_semantics=("parallel", "parallel"), vmem_limit_bytes=VMEM_LIMIT),
        name="out_mlp",
    )(x, yh, yg, mod, wo, nm, w1, w2, nf)


def _lag_reverse(a):
    return jnp.concatenate([a[0:1], a[:0:-1]], axis=0)


def _pair(w):
    z = jnp.zeros_like(w)
    return jnp.block([[w, z], [z, w]])


def _filter_features():
    length = SEQ
    t = jnp.linspace(0.0, 1.0, length, dtype=F32)[:, None]
    w = 2.0 * math.pi * jnp.arange(length, dtype=F32) / length
    f = jnp.linspace(1e-4, FILTER_BANDS - 1, FILTER_BANDS, dtype=F32)
    ang = w[:, None] * f[None, :]
    z = jnp.concatenate([t, jnp.cos(ang), -jnp.sin(ang)], axis=-1)
    z = jnp.pad(z, ((0, 0), (0, 128 - FILTER_EMB)))
    zp = jnp.concatenate([z[:length // 2], z[length // 2:]], axis=1)
    t2 = jnp.broadcast_to(jnp.stack([t, _lag_reverse(t)]), (2, length, 128))
    min_decay = math.log(DECAY_TARGET) / SLOW_DECAY_PCT
    max_decay = math.log(DECAY_TARGET) / FAST_DECAY_PCT
    deltas = jnp.abs(jnp.linspace(min_decay, max_decay, D_HYENA, dtype=F32))[None, :]
    return zp, t2, deltas


def _dft_tables():
    n2 = jnp.arange(FFT_N2, dtype=jnp.int32)
    k1 = jnp.arange(FFT_K1, dtype=jnp.int32)
    ang_tw = (2.0 * math.pi / FFT_N) * (k1[:, None] * n2[None, :]).astype(F32)
    twr = jnp.broadcast_to(jnp.cos(ang_tw)[:, :, None], (FFT_K1, FFT_N2, 128))
    twi = jnp.broadcast_to(-jnp.sin(ang_tw)[:, :, None], (FFT_K1, FFT_N2, 128))
    ang = (2.0 * math.pi / FFT_N2) * ((n2[:, None] * n2[None, :]) % FFT_N2).astype(F32)
    cr, ci = jnp.cos(ang), -jnp.sin(ang)
    fwd = jnp.block([[cr, -ci], [ci, cr]])
    inv = jnp.block([[cr, ci], [-ci, cr]])
    return fwd, inv, twr, twi


def kernel(x, c, ctx, c_ctx, w_ada, b_ada, norm_mix, norm_mlp, w_in, conv_w, conv_b, filt_w1, filt_b1, filt_w2, filt_b2, filt_w3, filt_b3, filt_freq, filt_wout, hyena_bias, gk_w_fwd, gk_b_fwd, gk_w_bwd, gk_b_bwd, gla_norm, w_out, w_mlp1, w_mlp2, norm_final):
    bsz = x.shape[0]
    l = 0
    cc = jnp.concatenate([c, c_ctx[None, :], jnp.zeros((16 - bsz - 1, D_MODEL), F32)], axis=0)
    mod_all = _mod_call(cc, w_ada[l], b_ada[l][None, :])
    mod = mod_all[:bsz][:, None, :]
    mod_c = jnp.broadcast_to(mod_all[bsz][None, None, :], (bsz, 1, 6 * D_MODEL))

    w_in_p = jnp.pad(w_in[l], ((0, 0), (0, N_IN_PAD - N_IN))).astype(BF16)
    gw = jnp.zeros((N_IN_PAD - GATE_COL, 2 * GLA_DK), F32)
    gw = gw.at[0:GATE_RANK, 0:GLA_DK].set(gk_w_fwd[l])
    gw = gw.at[GATE_RANK:2 * GATE_RANK, GLA_DK:].set(gk_w_bwd[l])
    gb = jnp.concatenate([gk_b_fwd[l], gk_b_bwd[l]])[None, :]
    gw = gw.astype(BF16)
    g_mix = norm_mix[l][None, :]

    conv = (conv_w[l], conv_b[l][None, :])
    x0c, u, q, k, v, og, lgf, lgb = _in_proj_call(x, mod, g_mix, w_in_p, gw, gb, conv, 512)
    k_c, v_c, lgf_c, lgb_c = _in_proj_call(ctx, mod_c, g_mix, w_in_p, gw, gb, None, ctx.shape[1])

    s_f, s_b = _ctx_state_call(k_c, v_c, lgf_c, lgb_c)
    o_f = _gla_call(q, k, v, lgf, s_f, None, False, 1024)
    y_gla = _gla_call(q, k, v, lgb, s_b, (o_f, og, gla_norm[l][None, :]), True, 1024)

    zp, t2, deltas = _filter_features()
    f2_fwd, f2_inv, twr, twi = _dft_tables()
    def two(b):
        return jnp.concatenate([b, b])[None, :]
    w1p = jnp.pad(filt_w1[l], ((0, 128 - FILTER_EMB), (0, 0)))
    hp = _filter_mlp_call(zp, _pair(w1p), two(filt_b1[l]), _pair(filt_w2[l]), two(filt_b2[l]),
                          _pair(filt_w3[l]), two(filt_b3[l]), two(filt_freq[l]))
    hdn = jnp.concatenate([hp[:, :FILTER_HIDDEN], hp[:, FILTER_HIDDEN:]], axis=0)
    hdn2 = jnp.stack([hdn, _lag_reverse(hdn)])
    khat = _filter_call(hdn2, *_split_bf16(filt_wout[l]), t2, deltas, *_split_bf16(f2_fwd), twr, twi)
    y_hy = _hyena_call(x0c, u, hyena_bias[l][None, :], khat,
                       f2_fwd.astype(BF16), f2_inv.astype(BF16), twr, twi)

    return _out_call(x, y_hy, y_gla, mod, w_out[l].astype(BF16), norm_mlp[l][None, :],
                     w_mlp1[l].astype(BF16), w_mlp2[l].astype(BF16), norm_final[None, :], 512)
```

```python
import functools
import math

import jax
import jax.numpy as jnp
from jax import lax
from jax.experimental import pallas as pl
from jax.experimental.pallas import tpu as pltpu

D_MODEL = 1024
SEQ = 4096
GRID_W = 64
D_HYENA = 512
GLA_HEADS = 4
GLA_DK = 256
GLA_DV = 512
GLA_HEAD_DK = 64
GLA_HEAD_DV = 128
GATE_RANK = 16
GATE_NORMALIZER = 16.0
GLA_CHUNK = 64
FILTER_EMB = 33
FILTER_BANDS = 16
FILTER_HIDDEN = 64
FAST_DECAY_PCT = 0.3
SLOW_DECAY_PCT = 1.5
DECAY_TARGET = 1e-2
D_FF = 4 * D_MODEL
EPS = 1e-6
N_IN = 3104
N_IN_PAD = 3200
GATE_COL = 3072

FFT_N = 2 * SEQ
FFT_N1 = 32
FFT_N2 = 256
FFT_K1 = FFT_N1 // 2 + 1
CG = 256
ROWS = 16

F32 = jnp.float32
BF16 = jnp.bfloat16
HI = lax.Precision.HIGHEST
VMEM_LIMIT = 56 * 1024 * 1024


def _dot(a, b, precision=None):
    return jnp.dot(a, b, preferred_element_type=F32, precision=precision)


def _add(a, b):
    if a is None:
        return b
    if b is None:
        return a
    return a + b


def _sub(a, b):
    if b is None:
        return a
    if a is None:
        return -b
    return a - b


def _scale(a, c):
    if a is None or c == 0.0:
        return None
    if c == 1.0:
        return a
    if c == -1.0:
        return -a
    return a * c


def _cadd(x, y):
    return (_add(x[0], y[0]), _add(x[1], y[1]))


def _csub(x, y):
    return (_sub(x[0], y[0]), _sub(x[1], y[1]))


def _conj(x):
    return (x[0], None if x[1] is None else -x[1])


def _snap(v):
    for t in (0.0, 1.0, -1.0):
        if abs(v - t) < 1e-12:
            return t
    return v


def _cmul_const(x, ang):
    c, s = _snap(math.cos(ang)), _snap(math.sin(ang))
    re = _sub(_scale(x[0], c), _scale(x[1], s))
    im = _add(_scale(x[0], s), _scale(x[1], c))
    return (re, im)


def _rfft_sym(x):
    n = len(x)
    if n == 1:
        return [(x[0], None)]
    if n == 2:
        return [(_add(x[0], x[1]), None), (_sub(x[0], x[1]), None)]
    ev = _rfft_sym(x[0::2])
    od = _rfft_sym(x[1::2])
    out = [None] * (n // 2 + 1)
    for k in range(n // 4 + 1):
        t = _cmul_const(od[k], -2.0 * math.pi * k / n)
        out[k] = _cadd(ev[k], t)
        if n // 2 - k != k:
            out[n // 2 - k] = _conj(_csub(ev[k], t))
    out[0] = (out[0][0], None)
    out[n // 2] = (out[n // 2][0], None)
    return out


class _Lin:
    __slots__ = ("parents",)

    def __init__(self, parents=()):
        self.parents = tuple(parents)

    def __add__(self, o):
        return _Lin(((1.0, self), (1.0, o)))

    def __sub__(self, o):
        return _Lin(((1.0, self), (-1.0, o)))

    def __neg__(self):
        return _Lin(((-1.0, self),))

    def __mul__(self, c):
        return _Lin(((float(c), self),))


def _fold_terms(terms):
    pos = [a for s, a in terms if s > 0]
    neg = [a for s, a in terms if s < 0]
    if pos:
        r = pos[0]
        for a in pos[1:]:
            r = r + a
        for a in neg:
            r = r - a
        return 1.0, r
    r = neg[0]
    for a in neg[1:]:
        r = r + a
    return -1.0, r


def _transpose_apply(inputs, outputs, cots):
    order, seen = [], set()

    def visit(n):
        if id(n) in seen:
            return
        seen.add(id(n))
        for _, p in n.parents:
            visit(p)
        order.append(n)

    acc = {}
    for o, g in zip(outputs, cots):
        if o is not None:
            visit(o)
            acc.setdefault(id(o), []).append((1.0, g))
    res = {}
    for n in reversed(order):
        terms = acc.pop(id(n), None)
        if terms is None:
            continue
        sign, g = _fold_terms(terms)
        if not n.parents:
            res[id(n)] = g if sign > 0 else -g
            continue
        for c, p in n.parents:
            c = c * sign
            acc.setdefault(id(p), []).append((c, g) if abs(c) == 1.0 else (1.0, g * c))
    return [res[id(i)] for i in inputs]


def _slow_forward(load_slab, n_slabs, a_ref, width):
    def body(i, carry):
        r = pl.multiple_of(i * ROWS, ROWS)
        xs = [load_slab(s, r) for s in range(n_slabs)] + [None] * (FFT_N1 - n_slabs)
        spec = _rfft_sym(xs)
        zero = jnp.zeros((ROWS, width), F32)
        for k1 in range(FFT_K1):
            re, im = spec[k1]
            a_ref[k1, pl.ds(r, ROWS), :] = zero if re is None else re
            a_ref[k1, pl.ds(FFT_N2 + r, ROWS), :] = zero if im is None else im
        return carry
    lax.fori_loop(0, FFT_N2 // ROWS, body, 0)


def _tile_lanes(t, width):
    reps = width // t.shape[-1]
    return t if reps == 1 else jnp.concatenate([t] * reps, axis=-1)


def _mod_kernel(c_ref, w_ref, b_ref, o_ref):
    c = c_ref[...]
    s = c * jax.nn.sigmoid(c)
    o_ref[...] = _dot(s, w_ref[...], HI) + b_ref[...]


def _mod_call(cc, w, b):
    n = w.shape[1]
    bn = 1024
    return pl.pallas_call(
        _mod_kernel,
        grid=(n // bn,),
        in_specs=[pl.BlockSpec(cc.shape, lambda j: (0, 0)),
                  pl.BlockSpec((D_MODEL, bn), lambda j: (0, j)),
                  pl.BlockSpec((1, bn), lambda j: (0, j))],
        out_specs=pl.BlockSpec((cc.shape[0], bn), lambda j: (0, j)),
        out_shape=jax.ShapeDtypeStruct((cc.shape[0], n), F32),
        compiler_params=pltpu.CompilerParams(vmem_limit_bytes=VMEM_LIMIT),
        name="mod",
    )(cc, w, b)


def _rms(x):
    return x * lax.rsqrt(jnp.mean(x * x, axis=-1, keepdims=True) + EPS)


def _short_conv(x, w, b):
    n = x.shape[0]
    pos = lax.broadcasted_iota(jnp.int32, x.shape, 0) & (GRID_W - 1)
    prev = jnp.where(pos == 0, 0.0, pltpu.roll(x, 1, 0))
    nxt = jnp.where(pos == GRID_W - 1, 0.0, pltpu.roll(x, n - 1, 0))
    return b + w[0:1, :] * prev + w[1:2, :] * x + w[2:3, :] * nxt


def _in_proj_kernel(*refs, latent):
    if latent:
        (x_ref, mod_ref, g_ref, w_ref, gw_ref, gb_ref, cw_ref, cb_ref,
         x0_ref, u_ref, q_ref, k_ref, v_ref, og_ref, lgf_ref, lgb_ref) = refs
    else:
        x_ref, mod_ref, g_ref, w_ref, gw_ref, gb_ref, k_ref, v_ref, lgf_ref, lgb_ref = refs
    x = x_ref[0]
    sh = mod_ref[0, :, 0:D_MODEL]
    sc = mod_ref[0, :, D_MODEL:2 * D_MODEL]
    h = ((_rms(x) * g_ref[...]) * (1.0 + sc) + sh).astype(BF16)
    c0 = 3 * D_HYENA
    c1 = c0 + GLA_DK
    c2 = c1 + GLA_DK
    c3 = c2 + GLA_DV
    gr = _dot(h, w_ref[:, GATE_COL:N_IN_PAD]).astype(BF16)
    z = _dot(gr, gw_ref[...]) + gb_ref[...]
    lg = jax.nn.log_sigmoid(z) * (1.0 / GATE_NORMALIZER)
    lgf_ref[0] = lg[:, 0:GLA_DK]
    lgb_ref[0] = lg[:, GLA_DK:2 * GLA_DK]
    if latent:
        dh = D_HYENA
        conv = [_short_conv(_dot(h, w_ref[:, j * dh:(j + 1) * dh]),
                            cw_ref[:, j * dh:(j + 1) * dh], cb_ref[:, j * dh:(j + 1) * dh])
                for j in range(3)]
        x0_ref[0] = conv[0].astype(BF16)
        u_ref[0] = (conv[1] * conv[2]).astype(BF16)
        q = _dot(h, w_ref[:, c0:c1]) * (GLA_HEAD_DK ** -0.5)
        q_ref[0] = q.astype(BF16)
        og_ref[0] = _dot(h, w_ref[:, c3:c3 + GLA_DV]).astype(BF16)
    k_ref[0] = _dot(h, w_ref[:, c1:c2]).astype(BF16)
    v_ref[0] = _dot(h, w_ref[:, c2:c3]).astype(BF16)


def _in_proj_call(x, mod, g, w, gw, gb, conv, tm):
    bsz, t, _ = x.shape
    latent = conv is not None
    def tile(width):
        return pl.BlockSpec((1, tm, width), lambda b, i: (b, i, 0))
    def whole(a):
        return pl.BlockSpec(a.shape, lambda b, i: (0,) * a.ndim)
    if latent:
        widths = (D_HYENA, D_HYENA, GLA_DK, GLA_DK, GLA_DV, GLA_DV, GLA_DK, GLA_DK)
        dtypes = (BF16,) * 6 + (F32, F32)
    else:
        widths = (GLA_DK, GLA_DV, GLA_DK, GLA_DK)
        dtypes = (BF16, BF16, F32, F32)
    args = [x, mod, g, w, gw, gb] + (list(conv) if latent else [])
    return pl.pallas_call(
        functools.partial(_in_proj_kernel, latent=latent),
        grid=(bsz, t // tm),
        in_specs=[tile(D_MODEL),
                  pl.BlockSpec((1, 1, 6 * D_MODEL), lambda b, i: (b, 0, 0))]
                 + [whole(a) for a in args[2:]],
        out_specs=[tile(wd) for wd in widths],
        out_shape=[jax.ShapeDtypeStruct((bsz, t, wd), dt) for wd, dt in zip(widths, dtypes)],
        compiler_params=pltpu.CompilerParams(
            dimension_semantics=("parallel", "parallel"), vmem_limit_bytes=VMEM_LIMIT),
        name="in_proj" if latent else "in_proj_ctx",
    )(*args)


def _ctx_state_kernel(k_ref, v_ref, lgf_ref, lgb_ref, sf_ref, sb_ref):
    t = k_ref.shape[1]
    row = lax.broadcasted_iota(jnp.int32, (t, t), 0)
    col = lax.broadcasted_iota(jnp.int32, (t, t), 1)
    k = k_ref[0].astype(F32)
    v = v_ref[0]
    for lg_ref, tri, o_ref in ((lgf_ref, col > row, sf_ref), (lgb_ref, col < row, sb_ref)):
        rem = _dot(tri.astype(F32), lg_ref[0], HI)
        kend = (k * jnp.exp(rem)).astype(BF16)
        ds = lax.dot_general(kend, v, (((0,), (0,)), ((), ())), preferred_element_type=F32)
        for h in range(GLA_HEADS):
            o_ref[0, h] = ds[h * GLA_HEAD_DK:(h + 1) * GLA_HEAD_DK,
                             h * GLA_HEAD_DV:(h + 1) * GLA_HEAD_DV]


def _ctx_state_call(k, v, lgf, lgb):
    bsz, t, _ = k.shape
    def tile(width):
        return pl.BlockSpec((1, t, width), lambda b: (b, 0, 0))
    st = pl.BlockSpec((1, GLA_HEADS, GLA_HEAD_DK, GLA_HEAD_DV), lambda b: (b, 0, 0, 0))
    return pl.pallas_call(
        _ctx_state_kernel,
        grid=(bsz,),
        in_specs=[tile(GLA_DK), tile(GLA_DV), tile(GLA_DK), tile(GLA_DK)],
        out_specs=[st, st],
        out_shape=[jax.ShapeDtypeStruct((bsz, GLA_HEADS, GLA_HEAD_DK, GLA_HEAD_DV), F32)] * 2,
        compiler_params=pltpu.CompilerParams(vmem_limit_bytes=VMEM_LIMIT),
        name="gla_ctx",
    )(k, v, lgf, lgb)


def _prefix_rows(x, reverse):
    n = x.shape[0]
    pos = lax.broadcasted_iota(jnp.int32, x.shape, 0) & (GLA_CHUNK - 1)
    k = 1
    while k < GLA_CHUNK:
        if reverse:
            x = x + jnp.where(pos < GLA_CHUNK - k, pltpu.roll(x, n - k, 0), 0.0)
        else:
            x = x + jnp.where(pos >= k, pltpu.roll(x, k, 0), 0.0)
        k *= 2
    return x


def _pair_diag(a, b):
    z = jnp.zeros_like(a)
    return jnp.concatenate([jnp.concatenate([a, z], axis=1),
                            jnp.concatenate([z, b], axis=1)], axis=0)


def _gla_stage1(q, k, v, b, reverse):
    c, hd, hv = GLA_CHUNK, GLA_HEAD_DK, GLA_HEAD_DV
    total = b[0:1] if reverse else b[c - 1:c]
    qin = (q * jnp.exp(b)).astype(BF16)
    kin = k * jnp.exp(-b)
    kend = k * jnp.exp(total - b)
    dec = jnp.exp(total)
    hi = dec.astype(BF16).astype(F32)
    mid = (dec - hi).astype(BF16).astype(F32)
    lo = dec - hi - mid
    r8 = lax.broadcasted_iota(jnp.int32, (8, GLA_DK), 0)
    extra = jnp.where(r8 == 0, hi, jnp.where(r8 == 1, mid, jnp.where(r8 == 2, lo, 0.0)))
    a = jnp.concatenate([kend, extra, jnp.zeros((c - 8, GLA_DK), F32)], axis=0)
    at = a.T.astype(BF16)
    r64 = lax.broadcasted_iota(jnp.int32, (c, 2 * hv), 0)
    l64 = lax.broadcasted_iota(jnp.int32, (c, 2 * hv), 1)
    ones_blk = jnp.where((r64 < 3) & (l64 >= hv), 1.0, 0.0).astype(BF16)
    zv = jnp.zeros((c, hv), BF16)
    key = lax.broadcasted_iota(jnp.int32, (c, 2 * hd), 1)
    qrow = lax.broadcasted_iota(jnp.int32, (c, 2 * hd), 0)
    att_mask = ((key & (hd - 1)) >= qrow) if reverse else ((key & (hd - 1)) <= qrow)
    atts = []
    for p in range(GLA_HEADS // 2):
        lanes = slice(p * 2 * hd, (p + 1) * 2 * hd)
        k2 = kin[:, lanes]
        kbd = jnp.concatenate([jnp.where(key < hd, k2, 0.0), jnp.where(key >= hd, k2, 0.0)],
                              axis=0).astype(BF16)
        att = lax.dot_general(qin[:, lanes], kbd, (((1,), (1,)), ((), ())),
                              preferred_element_type=F32)
        atts.append(jnp.where(att_mask, att, 0.0).astype(BF16))
    incs = []
    for h in range(GLA_HEADS):
        vh = v[:, h * hv:(h + 1) * hv]
        incs.append(_dot(at[h * hd:(h + 1) * hd, :],
                         jnp.concatenate([jnp.concatenate([vh, zv], axis=1), ones_blk], axis=0)))
    return qin, atts, incs


def _gla_stage2(qin, atts, incs, v, s):
    hd, hv = GLA_HEAD_DK, GLA_HEAD_DV
    outs = []
    for p in range(GLA_HEADS // 2):
        h0, h1 = 2 * p, 2 * p + 1
        rhs = jnp.concatenate([_pair_diag(v[:, h0 * hv:(h0 + 1) * hv], v[:, h1 * hv:(h1 + 1) * hv]),
                               _pair_diag(s[h0].astype(BF16), s[h1].astype(BF16))], axis=0)
        lhs = jnp.concatenate([atts[p], qin[:, p * 2 * hd:(p + 1) * 2 * hd]], axis=1)
        outs.append(_dot(lhs, rhs))
    s_new = [inc[:, hv:] * sh + inc[:, :hv] for inc, sh in zip(incs, s)]
    return jnp.concatenate(outs, axis=1), s_new


def _gla_kernel(*refs, reverse):
    if reverse:
        q_ref, k_ref, v_ref, lg_ref, s0_ref, of_ref, o_ref, s_scr, b_scr = refs
    else:
        q_ref, k_ref, v_ref, lg_ref, s0_ref, o_ref, s_scr, b_scr = refs
    c = GLA_CHUNK

    @pl.when(pl.program_id(1) == 0)
    def _():
        s_scr[...] = s0_ref[0]

    b_scr[...] = _prefix_rows(lg_ref[0], reverse)
    n_chunks = q_ref.shape[1] // c
    order = list(range(n_chunks - 1, -1, -1) if reverse else range(n_chunks))

    def start(ci):
        rows = slice(ci * c, (ci + 1) * c)
        return _gla_stage1(q_ref[0, rows, :].astype(F32), k_ref[0, rows, :].astype(F32),
                           v_ref[0, rows, :], b_scr[rows, :], reverse)

    def finish(ci, staged, s):
        rows = slice(ci * c, (ci + 1) * c)
        o, s = _gla_stage2(*staged, v_ref[0, rows, :], s)
        if reverse:
            o = o + of_ref[0, rows, :]
        o_ref[0, rows, :] = o
        return s

    s = [s_scr[h] for h in range(GLA_HEADS)]
    staged = start(order[0])
    for prev, ci in zip(order, order[1:]):
        nxt = start(ci)
        s = finish(prev, staged, s)
        staged = nxt
    s = finish(order[-1], staged, s)
    for h in range(GLA_HEADS):
        s_scr[h] = s[h]


def _gla_call(q, k, v, lg, s0, o_other, reverse, tt):
    bsz, t, _ = q.shape
    nt = t // tt
    if reverse:
        def tile(width):
            return pl.BlockSpec((1, tt, width), lambda b, j: (b, nt - 1 - j, 0))
    else:
        def tile(width):
            return pl.BlockSpec((1, tt, width), lambda b, j: (b, j, 0))
    in_specs = [tile(GLA_DK), tile(GLA_DK), tile(GLA_DV), tile(GLA_DK),
                pl.BlockSpec((1, GLA_HEADS, GLA_HEAD_DK, GLA_HEAD_DV), lambda b, j: (b, 0, 0, 0))]
    args = [q, k, v, lg, s0]
    if reverse:
        in_specs.append(tile(GLA_DV))
        args.append(o_other)
    return pl.pallas_call(
        functools.partial(_gla_kernel, reverse=reverse),
        grid=(bsz, nt),
        in_specs=in_specs,
        out_specs=tile(GLA_DV),
        out_shape=jax.ShapeDtypeStruct((bsz, t, GLA_DV), F32),
        scratch_shapes=[pltpu.VMEM((GLA_HEADS, GLA_HEAD_DK, GLA_HEAD_DV), F32),
                        pltpu.VMEM((tt, GLA_DK), F32)],
        compiler_params=pltpu.CompilerParams(
            dimension_semantics=("parallel", "arbitrary"), vmem_limit_bytes=VMEM_LIMIT),
        name="gla_bwd" if reverse else "gla_fwd",
    )(*args)


def _split_bf16(x):
    hi = x.astype(BF16)
    return hi, (x - hi.astype(F32)).astype(BF16)


def _dot_split(a_hi, a_lo, b_hi, b_lo):
    return _dot(a_hi, b_hi) + _dot(a_hi, b_lo) + _dot(a_lo, b_hi)


def _filter_mlp_kernel(z_ref, w1_ref, b1_ref, w2_ref, b2_ref, w3_ref, b3_ref, fr_ref, o_ref):
    fr = fr_ref[...]
    h = jnp.sin(fr * (_dot(z_ref[...], w1_ref[...], HI) + b1_ref[...]))
    h = jnp.sin(fr * (_dot(h, w2_ref[...], HI) + b2_ref[...]))
    o_ref[...] = jnp.sin(fr * (_dot(h, w3_ref[...], HI) + b3_ref[...]))


def _filter_mlp_call(zp, w1, b1, w2, b2, w3, b3, fr):
    return pl.pallas_call(
        _filter_mlp_kernel,
        out_shape=jax.ShapeDtypeStruct((zp.shape[0], 2 * FILTER_HIDDEN), F32),
        compiler_params=pltpu.CompilerParams(vmem_limit_bytes=VMEM_LIMIT),
        name="filter_mlp",
    )(zp, w1, b1, w2, b2, w3, b3, fr)


def _filter_kernel(hdn_ref, wfh_ref, wfl_ref, wbh_ref, wbl_ref, t_ref, dl_ref, f2h_ref, f2l_ref,
                   twr_ref, twi_ref, khat_ref, kern_ref, a_ref):
    dl = dl_ref[...]
    blk = 512

    def taps_body(i, ss):
        r0 = pl.multiple_of(i * blk, blk)
        rows = pl.ds(r0, blk)
        hf = _dot_split(*_split_bf16(hdn_ref[0, rows, :]), wfh_ref[...], wfl_ref[...])
        hb = _dot_split(*_split_bf16(hdn_ref[1, rows, :]), wbh_ref[...], wbl_ref[...])
        hf = hf * jnp.exp(-_tile_lanes(t_ref[0, rows, :], CG) * dl)
        hb = hb * jnp.exp(-_tile_lanes(t_ref[1, rows, :], CG) * dl)
        lag = r0 + lax.broadcasted_iota(jnp.int32, hb.shape, 0)
        hb = jnp.where(lag == 0, 0.0, hb)
        kern_ref[rows, :] = hf
        kern_ref[pl.ds(SEQ + r0, blk), :] = hb
        return (ss + jnp.sum(hf * hf, axis=0, keepdims=True)
                + jnp.sum(hb * hb, axis=0, keepdims=True))
    ss = lax.fori_loop(0, SEQ // blk, taps_body, jnp.zeros((1, CG), F32))
    scale = lax.rsqrt(ss + EPS) * (1.0 / FFT_N)

    def load_slab(s, r):
        return kern_ref[pl.ds(s * FFT_N2 + r, ROWS), :]
    _slow_forward(load_slab, FFT_N1, a_ref, CG)

    def body(k1, carry):
        ar = a_ref[k1, 0:FFT_N2, :]
        ai = a_ref[k1, FFT_N2:2 * FFT_N2, :]
        twr = _tile_lanes(twr_ref[k1], CG)
        twi = _tile_lanes(twi_ref[k1], CG)
        a = jnp.concatenate([ar * twr - ai * twi, ar * twi + ai * twr], axis=0)
        x = _dot_split(f2h_ref[...], f2l_ref[...], *_split_bf16(a))
        herm = jnp.where((k1 == 0) | (k1 == FFT_K1 - 1), 1.0, 2.0)
        khat_ref[k1] = x * (scale * herm)
        return carry
    lax.fori_loop(0, FFT_K1, body, 0)


def _filter_call(hdn, wout_hi, wout_lo, t2, deltas, f2_hi, f2_lo, twr, twi):
    ng = D_HYENA // CG
    def whole(a):
        return pl.BlockSpec(a.shape, lambda g: (0,) * a.ndim)
    fwd_cols = pl.BlockSpec((FILTER_HIDDEN, CG), lambda g: (0, g))
    bwd_cols = pl.BlockSpec((FILTER_HIDDEN, CG), lambda g: (0, ng + g))
    return pl.pallas_call(
        _filter_kernel,
        grid=(ng,),
        in_specs=[whole(hdn), fwd_cols, fwd_cols, bwd_cols, bwd_cols,
                  whole(t2),
                  pl.BlockSpec((1, CG), lambda g: (0, g)),
                  whole(f2_hi), whole(f2_lo), whole(twr), whole(twi)],
        out_specs=pl.BlockSpec((FFT_K1, 2 * FFT_N2, CG), lambda g: (0, 0, g)),
        out_shape=jax.ShapeDtypeStruct((FFT_K1, 2 * FFT_N2, D_HYENA), F32),
        scratch_shapes=[pltpu.VMEM((FFT_N, CG), F32),
                        pltpu.VMEM((FFT_K1, 2 * FFT_N2, CG), F32)],
        compiler_params=pltpu.CompilerParams(vmem_limit_bytes=VMEM_LIMIT),
        name="filter_spec",
    )(hdn, wout_hi, wout_lo, wout_hi, wout_lo, t2, deltas, f2_hi, f2_lo, twr, twi)


def _freq_forward(k1, a_ref, ff_ref, twr_ref, twi_ref):
    ar = a_ref[k1, 0:FFT_N2, :]
    ai = a_ref[k1, FFT_N2:2 * FFT_N2, :]
    twr = _tile_lanes(twr_ref[k1], CG)
    twi = _tile_lanes(twi_ref[k1], CG)
    a = jnp.concatenate([ar * twr - ai * twi, ar * twi + ai * twr], axis=0).astype(BF16)
    return _dot(ff_ref[...], a)


def _freq_inverse(k1, x, a_ref, khat_ref, fi_ref, twr_ref, twi_ref):
    xr, xi = x[0:FFT_N2], x[FFT_N2:]
    kr = khat_ref[k1, 0:FFT_N2, :]
    ki = khat_ref[k1, FFT_N2:2 * FFT_N2, :]
    y = jnp.concatenate([xr * kr - xi * ki, xr * ki + xi * kr], axis=0).astype(BF16)
    bq = _dot(fi_ref[...], y)
    br, bi = bq[0:FFT_N2], bq[FFT_N2:]
    twr = _tile_lanes(twr_ref[k1], CG)
    twi = _tile_lanes(twi_ref[k1], CG)
    a_ref[k1, 0:FFT_N2, :] = br * twr + bi * twi
    a_ref[k1, FFT_N2:2 * FFT_N2, :] = bi * twr - br * twi


def _hyena_kernel(x0_ref, uin_ref, bias_ref, khat_ref, ff_ref, fi_ref, twr_ref, twi_ref,
                  o_ref, u_ref, a_ref):
    blk = FFT_N2

    def load_body(i, carry):
        rows = pl.ds(pl.multiple_of(i * blk, blk), blk)
        u_ref[rows, :] = uin_ref[0, rows, :].astype(F32)
        return carry
    lax.fori_loop(0, SEQ // blk, load_body, 0)

    def load_slab(s, r):
        return u_ref[pl.ds(s * FFT_N2 + r, ROWS), :]
    _slow_forward(load_slab, FFT_N1 // 2, a_ref, CG)

    def freq_pair(ks):
        xs = [_freq_forward(k1, a_ref, ff_ref, twr_ref, twi_ref) for k1 in ks]
        for k1, x in zip(ks, xs):
            _freq_inverse(k1, x, a_ref, khat_ref, fi_ref, twr_ref, twi_ref)

    def freq_body(i, carry):
        freq_pair((2 * i, 2 * i + 1))
        return carry
    lax.fori_loop(0, FFT_K1 // 2, freq_body, 0)
    freq_pair((FFT_K1 - 1,))

    bias = bias_ref[...]

    net_in = [_Lin() for _ in range(FFT_N1 // 2)]
    net_out = [part for pair in _rfft_sym(net_in + [None] * (FFT_N1 // 2)) for part in pair]

    def inv_body(i, carry):
        r = pl.multiple_of(i * ROWS, ROWS)
        cots = [a_ref[k1, pl.ds(half * FFT_N2 + r, ROWS), :] if node is not None else None
                for k1 in range(FFT_K1) for half, node in enumerate(net_out[2 * k1:2 * k1 + 2])]
        ys = _transpose_apply(net_in, net_out, cots)
        for s in range(FFT_N1 // 2):
            rows = pl.ds(s * FFT_N2 + r, ROWS)
            u_ref[rows, :] = ys[s] + bias * u_ref[rows, :]
        return carry
    lax.fori_loop(0, FFT_N2 // ROWS, inv_body, 0)

    def out_body(i, carry):
        rows = pl.ds(pl.multiple_of(i * blk, blk), blk)
        o_ref[0, rows, :] = (x0_ref[0, rows, :].astype(F32) * u_ref[rows, :]).astype(o_ref.dtype)
        return carry
    lax.fori_loop(0, SEQ // blk, out_body, 0)


def _hyena_call(x0c, u, bias, khat, ff, fi, twr, twi):
    bsz = u.shape[0]
    ng = D_HYENA // CG
    stream = pl.BlockSpec((1, SEQ, CG), lambda g, b: (b, 0, g))
    def whole(a):
        return pl.BlockSpec(a.shape, lambda g, b: (0,) * a.ndim)
    return pl.pallas_call(
        _hyena_kernel,
        grid=(ng, bsz),
        in_specs=[stream, stream,
                  pl.BlockSpec((1, CG), lambda g, b: (0, g)),
                  pl.BlockSpec((FFT_K1, 2 * FFT_N2, CG), lambda g, b: (0, 0, g)),
                  whole(ff), whole(fi), whole(twr), whole(twi)],
        out_specs=stream,
        out_shape=jax.ShapeDtypeStruct((bsz, SEQ, D_HYENA), BF16),
        scratch_shapes=[pltpu.VMEM((SEQ, CG), F32),
                        pltpu.VMEM((FFT_K1, 2 * FFT_N2, CG), F32)],
        compiler_params=pltpu.CompilerParams(
            dimension_semantics=("parallel", "parallel"), vmem_limit_bytes=VMEM_LIMIT),
        name="hyena",
    )(x0c, u, bias, khat, ff, fi, twr, twi)


def _out_kernel(x_ref, yh_ref, att_ref, og_ref, gn_ref, mod_ref, wo_ref, nm_ref, w1_ref, w2_ref,
                nf_ref, o_ref):
    d = D_MODEL
    g1 = mod_ref[0, :, 2 * d:3 * d]
    sh2 = mod_ref[0, :, 3 * d:4 * d]
    sc2 = mod_ref[0, :, 4 * d:5 * d]
    g2 = mod_ref[0, :, 5 * d:6 * d]
    mix = _dot(yh_ref[0], wo_ref[0:D_HYENA, :])
    att = att_ref[0]
    og = og_ref[0].astype(F32)
    heads = [_rms(att[:, h * GLA_HEAD_DV:(h + 1) * GLA_HEAD_DV]) * gn_ref[...] for h in range(GLA_HEADS)]
    yg = (jnp.concatenate(heads, axis=1) * (og * jax.nn.sigmoid(og))).astype(BF16)
    mix = mix + _dot(yg, wo_ref[D_HYENA:d, :])
    x1 = x_ref[0] + g1 * mix
    h2 = ((_rms(x1) * nm_ref[...]) * (1.0 + sc2) + sh2).astype(BF16)
    acc = jnp.zeros(x1.shape, F32)
    for j in range(D_FF // d):
        hid = _dot(h2, w1_ref[:, j * d:(j + 1) * d])
        hid = jnp.square(jnp.maximum(hid, 0.0)).astype(BF16)
        acc = acc + _dot(hid, w2_ref[j * d:(j + 1) * d, :])
    x2 = x1 + g2 * acc
    o_ref[0] = _rms(x2) * nf_ref[...]


def _out_call(x, yh, o_gla, og, gn, mod, wo, nm, w1, w2, nf, tm):
    bsz, t, _ = x.shape
    def tile(width):
        return pl.BlockSpec((1, tm, width), lambda b, i: (b, i, 0))
    def whole(a):
        return pl.BlockSpec(a.shape, lambda b, i: (0,) * a.ndim)
    return pl.pallas_call(
        _out_kernel,
        grid=(bsz, t // tm),
        in_specs=[tile(D_MODEL), tile(D_HYENA), tile(GLA_DV), tile(GLA_DV), whole(gn),
                  pl.BlockSpec((1, 1, 6 * D_MODEL), lambda b, i: (b, 0, 0)),
                  whole(wo), whole(nm), whole(w1), whole(w2), whole(nf)],
        out_specs=tile(D_MODEL),
        out_shape=jax.ShapeDtypeStruct((bsz, t, D_MODEL), F32),
        compiler_params=pltpu.CompilerParams(
            dimension_semantics=("parallel", "parallel"), vmem_limit_bytes=VMEM_LIMIT),
        name="out_mlp",
    )(x, yh, o_gla, og, gn, mod, wo, nm, w1, w2, nf)


def _lag_reverse(a):
    return jnp.concatenate([a[0:1], a[:0:-1]], axis=0)


def _pair(w):
    z = jnp.zeros_like(w)
    return jnp.block([[w, z], [z, w]])


def _filter_features():
    length = SEQ
    t = jnp.linspace(0.0, 1.0, length, dtype=F32)[:, None]
    w = 2.0 * math.pi * jnp.arange(length, dtype=F32) / length
    f = jnp.linspace(1e-4, FILTER_BANDS - 1, FILTER_BANDS, dtype=F32)
    ang = w[:, None] * f[None, :]
    z = jnp.concatenate([t, jnp.cos(ang), -jnp.sin(ang)], axis=-1)
    z = jnp.pad(z, ((0, 0), (0, 128 - FILTER_EMB)))
    zp = jnp.concatenate([z[:length // 2], z[length // 2:]], axis=1)
    t2 = jnp.broadcast_to(jnp.stack([t, _lag_reverse(t)]), (2, length, 128))
    min_decay = math.log(DECAY_TARGET) / SLOW_DECAY_PCT
    max_decay = math.log(DECAY_TARGET) / FAST_DECAY_PCT
    deltas = jnp.abs(jnp.linspace(min_decay, max_decay, D_HYENA, dtype=F32))[None, :]
    return zp, t2, deltas


def _dft_tables():
    n2 = jnp.arange(FFT_N2, dtype=jnp.int32)
    k1 = jnp.arange(FFT_K1, dtype=jnp.int32)
    ang_tw = (2.0 * math.pi / FFT_N) * (k1[:, None] * n2[None, :]).astype(F32)
    twr = jnp.broadcast_to(jnp.cos(ang_tw)[:, :, None], (FFT_K1, FFT_N2, 128))
    twi = jnp.broadcast_to(-jnp.sin(ang_tw)[:, :, None], (FFT_K1, FFT_N2, 128))
    ang = (2.0 * math.pi / FFT_N2) * ((n2[:, None] * n2[None, :]) % FFT_N2).astype(F32)
    cr, ci = jnp.cos(ang), -jnp.sin(ang)
    fwd = jnp.block([[cr, -ci], [ci, cr]])
    inv = jnp.block([[cr, ci], [-ci, cr]])
    return fwd, inv, twr, twi


def kernel(x, c, ctx, c_ctx, w_ada, b_ada, norm_mix, norm_mlp, w_in, conv_w, conv_b, filt_w1, filt_b1, filt_w2, filt_b2, filt_w3, filt_b3, filt_freq, filt_wout, hyena_bias, gk_w_fwd, gk_b_fwd, gk_w_bwd, gk_b_bwd, gla_norm, w_out, w_mlp1, w_mlp2, norm_final):
    bsz = x.shape[0]
    l = 0
    cc = jnp.concatenate([c, c_ctx[None, :], jnp.zeros((16 - bsz - 1, D_MODEL), F32)], axis=0)
    mod_all = _mod_call(cc, w_ada[l], b_ada[l][None, :])
    mod = mod_all[:bsz][:, None, :]
    mod_c = jnp.broadcast_to(mod_all[bsz][None, None, :], (bsz, 1, 6 * D_MODEL))

    w_in_p = jnp.pad(w_in[l], ((0, 0), (0, N_IN_PAD - N_IN))).astype(BF16)
    gw = jnp.zeros((N_IN_PAD - GATE_COL, 2 * GLA_DK), F32)
    gw = gw.at[0:GATE_RANK, 0:GLA_DK].set(gk_w_fwd[l])
    gw = gw.at[GATE_RANK:2 * GATE_RANK, GLA_DK:].set(gk_w_bwd[l])
    gb = jnp.concatenate([gk_b_fwd[l], gk_b_bwd[l]])[None, :]
    gw = gw.astype(BF16)
    g_mix = norm_mix[l][None, :]

    conv = (conv_w[l], conv_b[l][None, :])
    x0c, u, q, k, v, og, lgf, lgb = _in_proj_call(x, mod, g_mix, w_in_p, gw, gb, conv, 512)
    k_c, v_c, lgf_c, lgb_c = _in_proj_call(ctx, mod_c, g_mix, w_in_p, gw, gb, None, ctx.shape[1])

    s_f, s_b = _ctx_state_call(k_c, v_c, lgf_c, lgb_c)
    o_f = _gla_call(q, k, v, lgf, s_f, None, False, 1024)
    o_gla = _gla_call(q, k, v, lgb, s_b, o_f, True, 1024)

    zp, t2, deltas = _filter_features()
    f2_fwd, f2_inv, twr, twi = _dft_tables()
    def two(b):
        return jnp.concatenate([b, b])[None, :]
    w1p = jnp.pad(filt_w1[l], ((0, 128 - FILTER_EMB), (0, 0)))
    hp = _filter_mlp_call(zp, _pair(w1p), two(filt_b1[l]), _pair(filt_w2[l]), two(filt_b2[l]),
                          _pair(filt_w3[l]), two(filt_b3[l]), two(filt_freq[l]))
    hdn = jnp.concatenate([hp[:, :FILTER_HIDDEN], hp[:, FILTER_HIDDEN:]], axis=0)
    hdn2 = jnp.stack([hdn, _lag_reverse(hdn)])
    khat = _filter_call(hdn2, *_split_bf16(filt_wout[l]), t2, deltas, *_split_bf16(f2_fwd), twr, twi)
    y_hy = _hyena_call(x0c, u, hyena_bias[l][None, :], khat,
                       f2_fwd.astype(BF16), f2_inv.astype(BF16), twr, twi)

    return _out_call(x, y_hy, o_gla, og, gla_norm[l][None, :], mod, w_out[l].astype(BF16), norm_mlp[l][None, :],
                     w_mlp1[l].astype(BF16), w_mlp2[l].astype(BF16), norm_final[None, :], 512)
```

```python
import functools
import math

import jax
import jax.numpy as jnp
from jax import lax
from jax.experimental import pallas as pl
from jax.experimental.pallas import tpu as pltpu

D_MODEL = 1024
SEQ = 4096
GRID_W = 64
D_HYENA = 512
GLA_HEADS = 4
GLA_DK = 256
GLA_DV = 512
GLA_HEAD_DK = 64
GLA_HEAD_DV = 128
GATE_RANK = 16
GATE_NORMALIZER = 16.0
GLA_CHUNK = 64
FILTER_EMB = 33
FILTER_BANDS = 16
FILTER_HIDDEN = 64
FAST_DECAY_PCT = 0.3
SLOW_DECAY_PCT = 1.5
DECAY_TARGET = 1e-2
D_FF = 4 * D_MODEL
EPS = 1e-6
N_IN = 3104
N_IN_PAD = 3200
GATE_COL = 3072

FFT_N = 2 * SEQ
FFT_N1 = 32
FFT_N2 = 256
FFT_K1 = FFT_N1 // 2 + 1
CG = 256
ROWS = 16

F32 = jnp.float32
BF16 = jnp.bfloat16
HI = lax.Precision.HIGHEST
VMEM_LIMIT = 56 * 1024 * 1024


def _dot(a, b, precision=None):
    return jnp.dot(a, b, preferred_element_type=F32, precision=precision)


def _add(a, b):
    if a is None:
        return b
    if b is None:
        return a
    return a + b


def _sub(a, b):
    if b is None:
        return a
    if a is None:
        return -b
    return a - b


def _scale(a, c):
    if a is None or c == 0.0:
        return None
    if c == 1.0:
        return a
    if c == -1.0:
        return -a
    return a * c


def _cadd(x, y):
    return (_add(x[0], y[0]), _add(x[1], y[1]))


def _csub(x, y):
    return (_sub(x[0], y[0]), _sub(x[1], y[1]))


def _conj(x):
    return (x[0], None if x[1] is None else -x[1])


def _snap(v):
    for t in (0.0, 1.0, -1.0):
        if abs(v - t) < 1e-12:
            return t
    return v


def _cmul_const(x, ang):
    c, s = _snap(math.cos(ang)), _snap(math.sin(ang))
    re = _sub(_scale(x[0], c), _scale(x[1], s))
    im = _add(_scale(x[0], s), _scale(x[1], c))
    return (re, im)


def _rfft_sym(x):
    n = len(x)
    if n == 1:
        return [(x[0], None)]
    if n == 2:
        return [(_add(x[0], x[1]), None), (_sub(x[0], x[1]), None)]
    ev = _rfft_sym(x[0::2])
    od = _rfft_sym(x[1::2])
    out = [None] * (n // 2 + 1)
    for k in range(n // 4 + 1):
        t = _cmul_const(od[k], -2.0 * math.pi * k / n)
        out[k] = _cadd(ev[k], t)
        if n // 2 - k != k:
            out[n // 2 - k] = _conj(_csub(ev[k], t))
    out[0] = (out[0][0], None)
    out[n // 2] = (out[n // 2][0], None)
    return out


class _Lin:
    __slots__ = ("parents",)

    def __init__(self, parents=()):
        self.parents = tuple(parents)

    def __add__(self, o):
        return _Lin(((1.0, self), (1.0, o)))

    def __sub__(self, o):
        return _Lin(((1.0, self), (-1.0, o)))

    def __neg__(self):
        return _Lin(((-1.0, self),))

    def __mul__(self, c):
        return _Lin(((float(c), self),))


def _fold_terms(terms):
    pos = [a for s, a in terms if s > 0]
    neg = [a for s, a in terms if s < 0]
    if pos:
        r = pos[0]
        for a in pos[1:]:
            r = r + a
        for a in neg:
            r = r - a
        return 1.0, r
    r = neg[0]
    for a in neg[1:]:
        r = r + a
    return -1.0, r


def _transpose_apply(inputs, outputs, cots):
    order, seen = [], set()

    def visit(n):
        if id(n) in seen:
            return
        seen.add(id(n))
        for _, p in n.parents:
            visit(p)
        order.append(n)

    acc = {}
    for o, g in zip(outputs, cots):
        if o is not None:
            visit(o)
            acc.setdefault(id(o), []).append((1.0, g))
    res = {}
    for n in reversed(order):
        terms = acc.pop(id(n), None)
        if terms is None:
            continue
        sign, g = _fold_terms(terms)
        if not n.parents:
            res[id(n)] = g if sign > 0 else -g
            continue
        for c, p in n.parents:
            c = c * sign
            acc.setdefault(id(p), []).append((c, g) if abs(c) == 1.0 else (1.0, g * c))
    return [res[id(i)] for i in inputs]


def _slow_forward(load_slab, n_slabs, a_ref, width):
    def body(i, carry):
        r = pl.multiple_of(i * ROWS, ROWS)
        xs = [load_slab(s, r) for s in range(n_slabs)] + [None] * (FFT_N1 - n_slabs)
        spec = _rfft_sym(xs)
        zero = jnp.zeros((ROWS, width), F32)
        for k1 in range(FFT_K1):
            re, im = spec[k1]
            a_ref[k1, pl.ds(r, ROWS), :] = zero if re is None else re
            a_ref[k1, pl.ds(FFT_N2 + r, ROWS), :] = zero if im is None else im
        return carry
    lax.fori_loop(0, FFT_N2 // ROWS, body, 0)


def _tile_lanes(t, width):
    reps = width // t.shape[-1]
    return t if reps == 1 else jnp.concatenate([t] * reps, axis=-1)


def _mod_kernel(c_ref, w_ref, b_ref, o_ref):
    c = c_ref[...]
    s = c * jax.nn.sigmoid(c)
    o_ref[...] = _dot(s, w_ref[...], HI) + b_ref[...]


def _mod_call(cc, w, b):
    n = w.shape[1]
    bn = 1024
    return pl.pallas_call(
        _mod_kernel,
        grid=(n // bn,),
        in_specs=[pl.BlockSpec(cc.shape, lambda j: (0, 0)),
                  pl.BlockSpec((D_MODEL, bn), lambda j: (0, j)),
                  pl.BlockSpec((1, bn), lambda j: (0, j))],
        out_specs=pl.BlockSpec((cc.shape[0], bn), lambda j: (0, j)),
        out_shape=jax.ShapeDtypeStruct((cc.shape[0], n), F32),
        compiler_params=pltpu.CompilerParams(vmem_limit_bytes=VMEM_LIMIT),
        name="mod",
    )(cc, w, b)


def _rms(x):
    return x * lax.rsqrt(jnp.mean(x * x, axis=-1, keepdims=True) + EPS)


def _short_conv(x, w, b):
    n = x.shape[0]
    pos = lax.broadcasted_iota(jnp.int32, x.shape, 0) & (GRID_W - 1)
    prev = jnp.where(pos == 0, 0.0, pltpu.roll(x, 1, 0))
    nxt = jnp.where(pos == GRID_W - 1, 0.0, pltpu.roll(x, n - 1, 0))
    return b + w[0:1, :] * prev + w[1:2, :] * x + w[2:3, :] * nxt


def _in_proj_kernel(*refs, latent):
    if latent:
        (x_ref, mod_ref, g_ref, w_ref, gw_ref, gb_ref, cw_ref, cb_ref,
         x0_ref, u_ref, q_ref, k_ref, v_ref, og_ref, lgf_ref, lgb_ref) = refs
    else:
        x_ref, mod_ref, g_ref, w_ref, gw_ref, gb_ref, k_ref, v_ref, lgf_ref, lgb_ref = refs
    x = x_ref[0]
    sh = mod_ref[0, :, 0:D_MODEL]
    sc = mod_ref[0, :, D_MODEL:2 * D_MODEL]
    h = ((_rms(x) * g_ref[...]) * (1.0 + sc) + sh).astype(BF16)
    c0 = 3 * D_HYENA
    c1 = c0 + GLA_DK
    c2 = c1 + GLA_DK
    c3 = c2 + GLA_DV
    gr = _dot(h, w_ref[:, GATE_COL:N_IN_PAD]).astype(BF16)
    z = _dot(gr, gw_ref[...]) + gb_ref[...]
    lg = jax.nn.log_sigmoid(z) * (1.0 / GATE_NORMALIZER)
    lgf_ref[0] = lg[:, 0:GLA_DK]
    lgb_ref[0] = lg[:, GLA_DK:2 * GLA_DK]
    if latent:
        dh = D_HYENA
        conv = [_short_conv(_dot(h, w_ref[:, j * dh:(j + 1) * dh]),
                            cw_ref[:, j * dh:(j + 1) * dh], cb_ref[:, j * dh:(j + 1) * dh])
                for j in range(3)]
        x0_ref[0] = conv[0].astype(BF16)
        u_ref[0] = (conv[1] * conv[2]).astype(BF16)
        q = _dot(h, w_ref[:, c0:c1]) * (GLA_HEAD_DK ** -0.5)
        q_ref[0] = q.astype(BF16)
        og_ref[0] = _dot(h, w_ref[:, c3:c3 + GLA_DV]).astype(BF16)
    k_ref[0] = _dot(h, w_ref[:, c1:c2]).astype(BF16)
    v_ref[0] = _dot(h, w_ref[:, c2:c3]).astype(BF16)


def _in_proj_call(x, mod, g, w, gw, gb, conv, tm):
    bsz, t, _ = x.shape
    latent = conv is not None
    def tile(width):
        return pl.BlockSpec((1, tm, width), lambda b, i: (b, i, 0))
    def whole(a):
        return pl.BlockSpec(a.shape, lambda b, i: (0,) * a.ndim)
    if latent:
        widths = (D_HYENA, D_HYENA, GLA_DK, GLA_DK, GLA_DV, GLA_DV, GLA_DK, GLA_DK)
        dtypes = (BF16,) * 6 + (F32, F32)
    else:
        widths = (GLA_DK, GLA_DV, GLA_DK, GLA_DK)
        dtypes = (BF16, BF16, F32, F32)
    args = [x, mod, g, w, gw, gb] + (list(conv) if latent else [])
    return pl.pallas_call(
        functools.partial(_in_proj_kernel, latent=latent),
        grid=(bsz, t // tm),
        in_specs=[tile(D_MODEL),
                  pl.BlockSpec((1, 1, 6 * D_MODEL), lambda b, i: (b, 0, 0))]
                 + [whole(a) for a in args[2:]],
        out_specs=[tile(wd) for wd in widths],
        out_shape=[jax.ShapeDtypeStruct((bsz, t, wd), dt) for wd, dt in zip(widths, dtypes)],
        compiler_params=pltpu.CompilerParams(
            dimension_semantics=("parallel", "parallel"), vmem_limit_bytes=VMEM_LIMIT),
        name="in_proj" if latent else "in_proj_ctx",
    )(*args)


def _ctx_state_kernel(k_ref, v_ref, lgf_ref, lgb_ref, sf_ref, sb_ref):
    t = k_ref.shape[1]
    row = lax.broadcasted_iota(jnp.int32, (t, t), 0)
    col = lax.broadcasted_iota(jnp.int32, (t, t), 1)
    k = k_ref[0].astype(F32)
    v = v_ref[0]
    for lg_ref, tri, o_ref in ((lgf_ref, col > row, sf_ref), (lgb_ref, col < row, sb_ref)):
        rem = _dot(tri.astype(F32), lg_ref[0], HI)
        kend = (k * jnp.exp(rem)).astype(BF16)
        ds = lax.dot_general(kend, v, (((0,), (0,)), ((), ())), preferred_element_type=F32)
        for h in range(GLA_HEADS):
            o_ref[0, h] = ds[h * GLA_HEAD_DK:(h + 1) * GLA_HEAD_DK,
                             h * GLA_HEAD_DV:(h + 1) * GLA_HEAD_DV]


def _ctx_state_call(k, v, lgf, lgb):
    bsz, t, _ = k.shape
    def tile(width):
        return pl.BlockSpec((1, t, width), lambda b: (b, 0, 0))
    st = pl.BlockSpec((1, GLA_HEADS, GLA_HEAD_DK, GLA_HEAD_DV), lambda b: (b, 0, 0, 0))
    return pl.pallas_call(
        _ctx_state_kernel,
        grid=(bsz,),
        in_specs=[tile(GLA_DK), tile(GLA_DV), tile(GLA_DK), tile(GLA_DK)],
        out_specs=[st, st],
        out_shape=[jax.ShapeDtypeStruct((bsz, GLA_HEADS, GLA_HEAD_DK, GLA_HEAD_DV), F32)] * 2,
        compiler_params=pltpu.CompilerParams(vmem_limit_bytes=VMEM_LIMIT),
        name="gla_ctx",
    )(k, v, lgf, lgb)


def _prefix_rows(x, reverse):
    n = x.shape[0]
    pos = lax.broadcasted_iota(jnp.int32, x.shape, 0) & (GLA_CHUNK - 1)
    k = 1
    while k < GLA_CHUNK:
        if reverse:
            x = x + jnp.where(pos < GLA_CHUNK - k, pltpu.roll(x, n - k, 0), 0.0)
        else:
            x = x + jnp.where(pos >= k, pltpu.roll(x, k, 0), 0.0)
        k *= 2
    return x


def _pair_diag(a, b):
    z = jnp.zeros_like(a)
    return jnp.concatenate([jnp.concatenate([a, z], axis=1),
                            jnp.concatenate([z, b], axis=1)], axis=0)


def _gla_stage1(q, k, v, b, reverse):
    c, hd, hv = GLA_CHUNK, GLA_HEAD_DK, GLA_HEAD_DV
    total = b[0:1] if reverse else b[c - 1:c]
    qin = (q * jnp.exp(b)).astype(BF16)
    kin = k * jnp.exp(-b)
    kend = k * jnp.exp(total - b)
    dec = jnp.exp(total)
    hi = dec.astype(BF16).astype(F32)
    mid = (dec - hi).astype(BF16).astype(F32)
    lo = dec - hi - mid
    r8 = lax.broadcasted_iota(jnp.int32, (8, GLA_DK), 0)
    extra = jnp.where(r8 == 0, hi, jnp.where(r8 == 1, mid, jnp.where(r8 == 2, lo, 0.0)))
    a = jnp.concatenate([kend, extra, jnp.zeros((c - 8, GLA_DK), F32)], axis=0)
    at = a.T.astype(BF16)
    r64 = lax.broadcasted_iota(jnp.int32, (c, 2 * hv), 0)
    l64 = lax.broadcasted_iota(jnp.int32, (c, 2 * hv), 1)
    ones_blk = jnp.where((r64 < 3) & (l64 >= hv), 1.0, 0.0).astype(BF16)
    zv = jnp.zeros((c, hv), BF16)
    key = lax.broadcasted_iota(jnp.int32, (c, 2 * hd), 1)
    qrow = lax.broadcasted_iota(jnp.int32, (c, 2 * hd), 0)
    att_mask = ((key & (hd - 1)) >= qrow) if reverse else ((key & (hd - 1)) <= qrow)
    atts = []
    for p in range(GLA_HEADS // 2):
        lanes = slice(p * 2 * hd, (p + 1) * 2 * hd)
        k2 = kin[:, lanes]
        kbd = jnp.concatenate([jnp.where(key < hd, k2, 0.0), jnp.where(key >= hd, k2, 0.0)],
                              axis=0).astype(BF16)
        att = lax.dot_general(qin[:, lanes], kbd, (((1,), (1,)), ((), ())),
                              preferred_element_type=F32)
        atts.append(jnp.where(att_mask, att, 0.0).astype(BF16))
    incs = []
    for h in range(GLA_HEADS):
        vh = v[:, h * hv:(h + 1) * hv]
        incs.append(_dot(at[h * hd:(h + 1) * hd, :],
                         jnp.concatenate([jnp.concatenate([vh, zv], axis=1), ones_blk], axis=0)))
    return qin, atts, incs


def _gla_stage2(qin, atts, incs, v, s):
    hd, hv = GLA_HEAD_DK, GLA_HEAD_DV
    outs = []
    for p in range(GLA_HEADS // 2):
        h0, h1 = 2 * p, 2 * p + 1
        rhs = jnp.concatenate([_pair_diag(v[:, h0 * hv:(h0 + 1) * hv], v[:, h1 * hv:(h1 + 1) * hv]),
                               _pair_diag(s[h0].astype(BF16), s[h1].astype(BF16))], axis=0)
        lhs = jnp.concatenate([atts[p], qin[:, p * 2 * hd:(p + 1) * 2 * hd]], axis=1)
        outs.append(_dot(lhs, rhs))
    s_new = [inc[:, hv:] * sh + inc[:, :hv] for inc, sh in zip(incs, s)]
    return jnp.concatenate(outs, axis=1), s_new


def _gla_kernel(*refs, reverse):
    if reverse:
        q_ref, k_ref, v_ref, lg_ref, s0_ref, of_ref, og_ref, gn_ref, o_ref, s_scr, b_scr = refs
    else:
        q_ref, k_ref, v_ref, lg_ref, s0_ref, o_ref, s_scr, b_scr = refs
    c = GLA_CHUNK

    @pl.when(pl.program_id(1) == 0)
    def _():
        s_scr[...] = s0_ref[0]

    b_scr[...] = _prefix_rows(lg_ref[0], reverse)
    n_chunks = q_ref.shape[1] // c
    order = list(range(n_chunks - 1, -1, -1) if reverse else range(n_chunks))

    def start(ci):
        rows = slice(ci * c, (ci + 1) * c)
        return _gla_stage1(q_ref[0, rows, :].astype(F32), k_ref[0, rows, :].astype(F32),
                           v_ref[0, rows, :], b_scr[rows, :], reverse)

    def finish(ci, staged, s):
        rows = slice(ci * c, (ci + 1) * c)
        o, s = _gla_stage2(*staged, v_ref[0, rows, :], s)
        if reverse:
            o = o + of_ref[0, rows, :]
            og = og_ref[0, rows, :].astype(F32)
            parts = []
            for h in range(GLA_HEADS):
                oh = o[:, h * GLA_HEAD_DV:(h + 1) * GLA_HEAD_DV]
                parts.append(_rms(oh) * gn_ref[...])
            y = jnp.concatenate(parts, axis=1) * (og * jax.nn.sigmoid(og))
            o_ref[0, rows, :] = y.astype(o_ref.dtype)
        else:
            o_ref[0, rows, :] = o
        return s

    s = [s_scr[h] for h in range(GLA_HEADS)]
    staged = start(order[0])
    for prev, ci in zip(order, order[1:]):
        nxt = start(ci)
        s = finish(prev, staged, s)
        staged = nxt
    s = finish(order[-1], staged, s)
    for h in range(GLA_HEADS):
        s_scr[h] = s[h]


def _gla_call(q, k, v, lg, s0, extra, reverse, tt):
    bsz, t, _ = q.shape
    nt = t // tt
    if reverse:
        def tile(width):
            return pl.BlockSpec((1, tt, width), lambda b, j: (b, nt - 1 - j, 0))
    else:
        def tile(width):
            return pl.BlockSpec((1, tt, width), lambda b, j: (b, j, 0))
    in_specs = [tile(GLA_DK), tile(GLA_DK), tile(GLA_DV), tile(GLA_DK),
                pl.BlockSpec((1, GLA_HEADS, GLA_HEAD_DK, GLA_HEAD_DV), lambda b, j: (b, 0, 0, 0))]
    args = [q, k, v, lg, s0]
    if reverse:
        o_f, og, gn = extra
        in_specs += [tile(GLA_DV), tile(GLA_DV), pl.BlockSpec(gn.shape, lambda b, j: (0, 0))]
        args += [o_f, og, gn]
    return pl.pallas_call(
        functools.partial(_gla_kernel, reverse=reverse),
        grid=(bsz, nt),
        in_specs=in_specs,
        out_specs=tile(GLA_DV),
        out_shape=jax.ShapeDtypeStruct((bsz, t, GLA_DV), BF16 if reverse else F32),
        scratch_shapes=[pltpu.VMEM((GLA_HEADS, GLA_HEAD_DK, GLA_HEAD_DV), F32),
                        pltpu.VMEM((tt, GLA_DK), F32)],
        compiler_params=pltpu.CompilerParams(
            dimension_semantics=("parallel", "arbitrary"), vmem_limit_bytes=VMEM_LIMIT),
        name="gla_bwd" if reverse else "gla_fwd",
    )(*args)


def _split_bf16(x):
    hi = x.astype(BF16)
    return hi, (x - hi.astype(F32)).astype(BF16)


def _dot_split(a_hi, a_lo, b_hi, b_lo):
    return _dot(a_hi, b_hi) + _dot(a_hi, b_lo) + _dot(a_lo, b_hi)


def _filter_mlp_kernel(z_ref, w1_ref, b1_ref, w2_ref, b2_ref, w3_ref, b3_ref, fr_ref, o_ref):
    fr = fr_ref[...]
    h = jnp.sin(fr * (_dot(z_ref[...], w1_ref[...], HI) + b1_ref[...]))
    h = jnp.sin(fr * (_dot(h, w2_ref[...], HI) + b2_ref[...]))
    o_ref[...] = jnp.sin(fr * (_dot(h, w3_ref[...], HI) + b3_ref[...]))


def _filter_mlp_call(zp, w1, b1, w2, b2, w3, b3, fr):
    return pl.pallas_call(
        _filter_mlp_kernel,
        out_shape=jax.ShapeDtypeStruct((zp.shape[0], 2 * FILTER_HIDDEN), F32),
        compiler_params=pltpu.CompilerParams(vmem_limit_bytes=VMEM_LIMIT),
        name="filter_mlp",
    )(zp, w1, b1, w2, b2, w3, b3, fr)


def _filter_kernel(hdn_ref, wfh_ref, wfl_ref, wbh_ref, wbl_ref, t_ref, dl_ref, f2h_ref, f2l_ref,
                   twr_ref, twi_ref, khat_ref, kern_ref, a_ref):
    dl = dl_ref[...]
    blk = 512

    def taps_body(i, ss):
        r0 = pl.multiple_of(i * blk, blk)
        rows = pl.ds(r0, blk)
        hf = _dot_split(*_split_bf16(hdn_ref[0, rows, :]), wfh_ref[...], wfl_ref[...])
        hb = _dot_split(*_split_bf16(hdn_ref[1, rows, :]), wbh_ref[...], wbl_ref[...])
        hf = hf * jnp.exp(-_tile_lanes(t_ref[0, rows, :], CG) * dl)
        hb = hb * jnp.exp(-_tile_lanes(t_ref[1, rows, :], CG) * dl)
        lag = r0 + lax.broadcasted_iota(jnp.int32, hb.shape, 0)
        hb = jnp.where(lag == 0, 0.0, hb)
        kern_ref[rows, :] = hf
        kern_ref[pl.ds(SEQ + r0, blk), :] = hb
        return (ss + jnp.sum(hf * hf, axis=0, keepdims=True)
                + jnp.sum(hb * hb, axis=0, keepdims=True))
    ss = lax.fori_loop(0, SEQ // blk, taps_body, jnp.zeros((1, CG), F32))
    scale = lax.rsqrt(ss + EPS) * (1.0 / FFT_N)

    def load_slab(s, r):
        return kern_ref[pl.ds(s * FFT_N2 + r, ROWS), :]
    _slow_forward(load_slab, FFT_N1, a_ref, CG)

    def body(k1, carry):
        ar = a_ref[k1, 0:FFT_N2, :]
        ai = a_ref[k1, FFT_N2:2 * FFT_N2, :]
        twr = _tile_lanes(twr_ref[k1], CG)
        twi = _tile_lanes(twi_ref[k1], CG)
        a = jnp.concatenate([ar * twr - ai * twi, ar * twi + ai * twr], axis=0)
        x = _dot_split(f2h_ref[...], f2l_ref[...], *_split_bf16(a))
        herm = jnp.where((k1 == 0) | (k1 == FFT_K1 - 1), 1.0, 2.0)
        khat_ref[k1] = x * (scale * herm)
        return carry
    lax.fori_loop(0, FFT_K1, body, 0)


def _filter_call(hdn, wout_hi, wout_lo, t2, deltas, f2_hi, f2_lo, twr, twi):
    ng = D_HYENA // CG
    def whole(a):
        return pl.BlockSpec(a.shape, lambda g: (0,) * a.ndim)
    fwd_cols = pl.BlockSpec((FILTER_HIDDEN, CG), lambda g: (0, g))
    bwd_cols = pl.BlockSpec((FILTER_HIDDEN, CG), lambda g: (0, ng + g))
    return pl.pallas_call(
        _filter_kernel,
        grid=(ng,),
        in_specs=[whole(hdn), fwd_cols, fwd_cols, bwd_cols, bwd_cols,
                  whole(t2),
                  pl.BlockSpec((1, CG), lambda g: (0, g)),
                  whole(f2_hi), whole(f2_lo), whole(twr), whole(twi)],
        out_specs=pl.BlockSpec((FFT_K1, 2 * FFT_N2, CG), lambda g: (0, 0, g)),
        out_shape=jax.ShapeDtypeStruct((FFT_K1, 2 * FFT_N2, D_HYENA), F32),
        scratch_shapes=[pltpu.VMEM((FFT_N, CG), F32),
                        pltpu.VMEM((FFT_K1, 2 * FFT_N2, CG), F32)],
        compiler_params=pltpu.CompilerParams(vmem_limit_bytes=VMEM_LIMIT),
        name="filter_spec",
    )(hdn, wout_hi, wout_lo, wout_hi, wout_lo, t2, deltas, f2_hi, f2_lo, twr, twi)


def _freq_forward(k1, a_ref, ff_ref, twr_ref, twi_ref):
    ar = a_ref[k1, 0:FFT_N2, :]
    ai = a_ref[k1, FFT_N2:2 * FFT_N2, :]
    twr = _tile_lanes(twr_ref[k1], CG)
    twi = _tile_lanes(twi_ref[k1], CG)
    a = jnp.concatenate([ar * twr - ai * twi, ar * twi + ai * twr], axis=0).astype(BF16)
    return _dot(ff_ref[...], a)


def _freq_inverse(k1, x, a_ref, khat_ref, fi_ref, twr_ref, twi_ref):
    xr, xi = x[0:FFT_N2], x[FFT_N2:]
    kr = khat_ref[k1, 0:FFT_N2, :]
    ki = khat_ref[k1, FFT_N2:2 * FFT_N2, :]
    y = jnp.concatenate([xr * kr - xi * ki, xr * ki + xi * kr], axis=0).astype(BF16)
    bq = _dot(fi_ref[...], y)
    br, bi = bq[0:FFT_N2], bq[FFT_N2:]
    twr = _tile_lanes(twr_ref[k1], CG)
    twi = _tile_lanes(twi_ref[k1], CG)
    a_ref[k1, 0:FFT_N2, :] = br * twr + bi * twi
    a_ref[k1, FFT_N2:2 * FFT_N2, :] = bi * twr - br * twi


def _hyena_kernel(x0_ref, uin_ref, bias_ref, khat_ref, ff_ref, fi_ref, twr_ref, twi_ref,
                  o_ref, u_ref, a_ref):
    blk = FFT_N2

    def load_body(i, carry):
        rows = pl.ds(pl.multiple_of(i * blk, blk), blk)
        u_ref[rows, :] = uin_ref[0, rows, :].astype(F32)
        return carry
    lax.fori_loop(0, SEQ // blk, load_body, 0)

    def load_slab(s, r):
        return u_ref[pl.ds(s * FFT_N2 + r, ROWS), :]
    _slow_forward(load_slab, FFT_N1 // 2, a_ref, CG)

    def freq_pair(ks):
        xs = [_freq_forward(k1, a_ref, ff_ref, twr_ref, twi_ref) for k1 in ks]
        for k1, x in zip(ks, xs):
            _freq_inverse(k1, x, a_ref, khat_ref, fi_ref, twr_ref, twi_ref)

    def freq_body(i, carry):
        freq_pair((2 * i, 2 * i + 1))
        return carry
    lax.fori_loop(0, FFT_K1 // 2, freq_body, 0)
    freq_pair((FFT_K1 - 1,))

    bias = bias_ref[...]

    net_in = [_Lin() for _ in range(FFT_N1 // 2)]
    net_out = [part for pair in _rfft_sym(net_in + [None] * (FFT_N1 // 2)) for part in pair]

    def inv_body(i, carry):
        r = pl.multiple_of(i * ROWS, ROWS)
        cots = [a_ref[k1, pl.ds(half * FFT_N2 + r, ROWS), :] if node is not None else None
                for k1 in range(FFT_K1) for half, node in enumerate(net_out[2 * k1:2 * k1 + 2])]
        ys = _transpose_apply(net_in, net_out, cots)
        for s in range(FFT_N1 // 2):
            rows = pl.ds(s * FFT_N2 + r, ROWS)
            u_ref[rows, :] = ys[s] + bias * u_ref[rows, :]
        return carry
    lax.fori_loop(0, FFT_N2 // ROWS, inv_body, 0)

    def out_body(i, carry):
        rows = pl.ds(pl.multiple_of(i * blk, blk), blk)
        o_ref[0, rows, :] = (x0_ref[0, rows, :].astype(F32) * u_ref[rows, :]).astype(o_ref.dtype)
        return carry
    lax.fori_loop(0, SEQ // blk, out_body, 0)


def _hyena_call(x0c, u, bias, khat, ff, fi, twr, twi):
    bsz = u.shape[0]
    ng = D_HYENA // CG
    stream = pl.BlockSpec((1, SEQ, CG), lambda g, b: (b, 0, g))
    def whole(a):
        return pl.BlockSpec(a.shape, lambda g, b: (0,) * a.ndim)
    return pl.pallas_call(
        _hyena_kernel,
        grid=(ng, bsz),
        in_specs=[stream, stream,
                  pl.BlockSpec((1, CG), lambda g, b: (0, g)),
                  pl.BlockSpec((FFT_K1, 2 * FFT_N2, CG), lambda g, b: (0, 0, g)),
                  whole(ff), whole(fi), whole(twr), whole(twi)],
        out_specs=stream,
        out_shape=jax.ShapeDtypeStruct((bsz, SEQ, D_HYENA), BF16),
        scratch_shapes=[pltpu.VMEM((SEQ, CG), F32),
                        pltpu.VMEM((FFT_K1, 2 * FFT_N2, CG), F32)],
        compiler_params=pltpu.CompilerParams(
            dimension_semantics=("parallel", "parallel"), vmem_limit_bytes=VMEM_LIMIT),
        name="hyena",
    )(x0c, u, bias, khat, ff, fi, twr, twi)


def _out_kernel(x_ref, yh_ref, yg_ref, mod_ref, wo_ref, nm_ref, w1_ref, w2_ref, nf_ref, o_ref):
    d = D_MODEL
    g1 = mod_ref[0, :, 2 * d:3 * d]
    sh2 = mod_ref[0, :, 3 * d:4 * d]
    sc2 = mod_ref[0, :, 4 * d:5 * d]
    g2 = mod_ref[0, :, 5 * d:6 * d]
    mix = _dot(yh_ref[0], wo_ref[0:D_HYENA, :]) + _dot(yg_ref[0], wo_ref[D_HYENA:d, :])
    x1 = x_ref[0] + g1 * mix
    h2 = ((_rms(x1) * nm_ref[...]) * (1.0 + sc2) + sh2).astype(BF16)
    acc = jnp.zeros(x1.shape, F32)
    for j in range(D_FF // d):
        hid = _dot(h2, w1_ref[:, j * d:(j + 1) * d])
        hid = jnp.square(jnp.maximum(hid, 0.0)).astype(BF16)
        acc = acc + _dot(hid, w2_ref[j * d:(j + 1) * d, :])
    x2 = x1 + g2 * acc
    o_ref[0] = _rms(x2) * nf_ref[...]


def _out_call(x, yh, yg, mod, wo, nm, w1, w2, nf, tm):
    bsz, t, _ = x.shape
    def tile(width):
        return pl.BlockSpec((1, tm, width), lambda b, i: (b, i, 0))
    def whole(a):
        return pl.BlockSpec(a.shape, lambda b, i: (0,) * a.ndim)
    return pl.pallas_call(
        _out_kernel,
        grid=(bsz, t // tm),
        in_specs=[tile(D_MODEL), tile(D_HYENA), tile(GLA_DV),
                  pl.BlockSpec((1, 1, 6 * D_MODEL), lambda b, i: (b, 0, 0)),
                  whole(wo), whole(nm), whole(w1), whole(w2), whole(nf)],
        out_specs=tile(D_MODEL),
        out_shape=jax.ShapeDtypeStruct((bsz, t, D_MODEL), F32),
        compiler_params=pltpu.CompilerParams(
            dimension_semantics=("parallel", "parallel"), vmem_limit_bytes=VMEM_LIMIT),
        name="out_mlp",
    )(x, yh, yg, mod, wo, nm, w1, w2, nf)


def _lag_reverse(a):
    return jnp.concatenate([a[0:1], a[:0:-1]], axis=0)


def _pair(w):
    z = jnp.zeros_like(w)
    return jnp.block([[w, z], [z, w]])


def _filter_features():
    length = SEQ
    t = jnp.linspace(0.0, 1.0, length, dtype=F32)[:, None]
    w = 2.0 * math.pi * jnp.arange(length, dtype=F32) / length
    f = jnp.linspace(1e-4, FILTER_BANDS - 1, FILTER_BANDS, dtype=F32)
    ang = w[:, None] * f[None, :]
    z = jnp.concatenate([t, jnp.cos(ang), -jnp.sin(ang)], axis=-1)
    z = jnp.pad(z, ((0, 0), (0, 128 - FILTER_EMB)))
    zp = jnp.concatenate([z[:length // 2], z[length // 2:]], axis=1)
    t2 = jnp.broadcast_to(jnp.stack([t, _lag_reverse(t)]), (2, length, 128))
    min_decay = math.log(DECAY_TARGET) / SLOW_DECAY_PCT
    max_decay = math.log(DECAY_TARGET) / FAST_DECAY_PCT
    deltas = jnp.abs(jnp.linspace(min_decay, max_decay, D_HYENA, dtype=F32))[None, :]
    return zp, t2, deltas


def _dft_tables():
    n2 = jnp.arange(FFT_N2, dtype=jnp.int32)
    k1 = jnp.arange(FFT_K1, dtype=jnp.int32)
    ang_tw = (2.0 * math.pi / FFT_N) * (k1[:, None] * n2[None, :]).astype(F32)
    twr = jnp.broadcast_to(jnp.cos(ang_tw)[:, :, None], (FFT_K1, FFT_N2, 128))
    twi = jnp.broadcast_to(-jnp.sin(ang_tw)[:, :, None], (FFT_K1, FFT_N2, 128))
    ang = (2.0 * math.pi / FFT_N2) * ((n2[:, None] * n2[None, :]) % FFT_N2).astype(F32)
    cr, ci = jnp.cos(ang), -jnp.sin(ang)
    fwd = jnp.block([[cr, -ci], [ci, cr]])
    inv = jnp.block([[cr, ci], [-ci, cr]])
    return fwd, inv, twr, twi


def kernel(x, c, ctx, c_ctx, w_ada, b_ada, norm_mix, norm_mlp, w_in, conv_w, conv_b, filt_w1, filt_b1, filt_w2, filt_b2, filt_w3, filt_b3, filt_freq, filt_wout, hyena_bias, gk_w_fwd, gk_b_fwd, gk_w_bwd, gk_b_bwd, gla_norm, w_out, w_mlp1, w_mlp2, norm_final):
    bsz = x.shape[0]
    l = 0
    cc = jnp.concatenate([c, c_ctx[None, :], jnp.zeros((16 - bsz - 1, D_MODEL), F32)], axis=0)
    mod_all = _mod_call(cc, w_ada[l], b_ada[l][None, :])
    mod = mod_all[:bsz][:, None, :]
    mod_c = jnp.broadcast_to(mod_all[bsz][None, None, :], (bsz, 1, 6 * D_MODEL))

    w_in_p = jnp.pad(w_in[l], ((0, 0), (0, N_IN_PAD - N_IN))).astype(BF16)
    gw = jnp.zeros((N_IN_PAD - GATE_COL, 2 * GLA_DK), F32)
    gw = gw.at[0:GATE_RANK, 0:GLA_DK].set(gk_w_fwd[l])
    gw = gw.at[GATE_RANK:2 * GATE_RANK, GLA_DK:].set(gk_w_bwd[l])
    gb = jnp.concatenate([gk_b_fwd[l], gk_b_bwd[l]])[None, :]
    gw = gw.astype(BF16)
    g_mix = norm_mix[l][None, :]

    conv = (conv_w[l], conv_b[l][None, :])
    x0c, u, q, k, v, og, lgf, lgb = _in_proj_call(x, mod, g_mix, w_in_p, gw, gb, conv, 512)
    k_c, v_c, lgf_c, lgb_c = _in_proj_call(ctx, mod_c, g_mix, w_in_p, gw, gb, None, ctx.shape[1])

    s_f, s_b = _ctx_state_call(k_c, v_c, lgf_c, lgb_c)
    o_f = _gla_call(q, k, v, lgf, s_f, None, False, 2048)
    y_gla = _gla_call(q, k, v, lgb, s_b, (o_f, og, gla_norm[l][None, :]), True, 2048)

    zp, t2, deltas = _filter_features()
    f2_fwd, f2_inv, twr, twi = _dft_tables()
    def two(b):
        return jnp.concatenate([b, b])[None, :]
    w1p = jnp.pad(filt_w1[l], ((0, 128 - FILTER_EMB), (0, 0)))
    hp = _filter_mlp_call(zp, _pair(w1p), two(filt_b1[l]), _pair(filt_w2[l]), two(filt_b2[l]),
                          _pair(filt_w3[l]), two(filt_b3[l]), two(filt_freq[l]))
    hdn = jnp.concatenate([hp[:, :FILTER_HIDDEN], hp[:, FILTER_HIDDEN:]], axis=0)
    hdn2 = jnp.stack([hdn, _lag_reverse(hdn)])
    khat = _filter_call(hdn2, *_split_bf16(filt_wout[l]), t2, deltas, *_split_bf16(f2_fwd), twr, twi)
    y_hy = _hyena_call(x0c, u, hyena_bias[l][None, :], khat,
                       f2_fwd.astype(BF16), f2_inv.astype(BF16), twr, twi)

    return _out_call(x, y_hy, y_gla, mod, w_out[l].astype(BF16), norm_mlp[l][None, :],
                     w_mlp1[l].astype(BF16), w_mlp2[l].astype(BF16), norm_final[None, :], 512)
```

```python
import functools
import math

import jax
import jax.numpy as jnp
from jax import lax
from jax.experimental import pallas as pl
from jax.experimental.pallas import tpu as pltpu

D_MODEL = 1024
SEQ = 4096
GRID_W = 64
D_HYENA = 512
GLA_HEADS = 4
GLA_DK = 256
GLA_DV = 512
GLA_HEAD_DK = 64
GLA_HEAD_DV = 128
GATE_RANK = 16
GATE_NORMALIZER = 16.0
GLA_CHUNK = 64
FILTER_EMB = 33
FILTER_BANDS = 16
FILTER_HIDDEN = 64
FAST_DECAY_PCT = 0.3
SLOW_DECAY_PCT = 1.5
DECAY_TARGET = 1e-2
D_FF = 4 * D_MODEL
EPS = 1e-6
N_IN = 3104
N_IN_PAD = 3200
GATE_COL = 3072

FFT_N = 2 * SEQ
FFT_N1 = 32
FFT_N2 = 256
FFT_K1 = FFT_N1 // 2 + 1
CG = 256
ROWS = 16

F32 = jnp.float32
BF16 = jnp.bfloat16
HI = lax.Precision.HIGHEST
VMEM_LIMIT = 56 * 1024 * 1024


def _dot(a, b, precision=None):
    return jnp.dot(a, b, preferred_element_type=F32, precision=precision)


def _add(a, b):
    if a is None:
        return b
    if b is None:
        return a
    return a + b


def _sub(a, b):
    if b is None:
        return a
    if a is None:
        return -b
    return a - b


def _scale(a, c):
    if a is None or c == 0.0:
        return None
    if c == 1.0:
        return a
    if c == -1.0:
        return -a
    return a * c


def _cadd(x, y):
    return (_add(x[0], y[0]), _add(x[1], y[1]))


def _csub(x, y):
    return (_sub(x[0], y[0]), _sub(x[1], y[1]))


def _conj(x):
    return (x[0], None if x[1] is None else -x[1])


def _snap(v):
    for t in (0.0, 1.0, -1.0):
        if abs(v - t) < 1e-12:
            return t
    return v


def _cmul_const(x, ang):
    c, s = _snap(math.cos(ang)), _snap(math.sin(ang))
    re = _sub(_scale(x[0], c), _scale(x[1], s))
    im = _add(_scale(x[0], s), _scale(x[1], c))
    return (re, im)


def _rfft_sym(x):
    n = len(x)
    if n == 1:
        return [(x[0], None)]
    if n == 2:
        return [(_add(x[0], x[1]), None), (_sub(x[0], x[1]), None)]
    ev = _rfft_sym(x[0::2])
    od = _rfft_sym(x[1::2])
    out = [None] * (n // 2 + 1)
    for k in range(n // 4 + 1):
        t = _cmul_const(od[k], -2.0 * math.pi * k / n)
        out[k] = _cadd(ev[k], t)
        if n // 2 - k != k:
            out[n // 2 - k] = _conj(_csub(ev[k], t))
    out[0] = (out[0][0], None)
    out[n // 2] = (out[n // 2][0], None)
    return out


class _Lin:
    __slots__ = ("parents",)

    def __init__(self, parents=()):
        self.parents = tuple(parents)

    def __add__(self, o):
        return _Lin(((1.0, self), (1.0, o)))

    def __sub__(self, o):
        return _Lin(((1.0, self), (-1.0, o)))

    def __neg__(self):
        return _Lin(((-1.0, self),))

    def __mul__(self, c):
        return _Lin(((float(c), self),))


def _fold_terms(terms):
    pos = [a for s, a in terms if s > 0]
    neg = [a for s, a in terms if s < 0]
    if pos:
        r = pos[0]
        for a in pos[1:]:
            r = r + a
        for a in neg:
            r = r - a
        return 1.0, r
    r = neg[0]
    for a in neg[1:]:
        r = r + a
    return -1.0, r


def _transpose_apply(inputs, outputs, cots):
    order, seen = [], set()

    def visit(n):
        if id(n) in seen:
            return
        seen.add(id(n))
        for _, p in n.parents:
            visit(p)
        order.append(n)

    acc = {}
    for o, g in zip(outputs, cots):
        if o is not None:
            visit(o)
            acc.setdefault(id(o), []).append((1.0, g))
    res = {}
    for n in reversed(order):
        terms = acc.pop(id(n), None)
        if terms is None:
            continue
        sign, g = _fold_terms(terms)
        if not n.parents:
            res[id(n)] = g if sign > 0 else -g
            continue
        for c, p in n.parents:
            c = c * sign
            acc.setdefault(id(p), []).append((c, g) if abs(c) == 1.0 else (1.0, g * c))
    return [res[id(i)] for i in inputs]


def _slow_forward(load_slab, n_slabs, a_ref, width):
    def body(i, carry):
        r = pl.multiple_of(i * ROWS, ROWS)
        xs = [load_slab(s, r) for s in range(n_slabs)] + [None] * (FFT_N1 - n_slabs)
        spec = _rfft_sym(xs)
        zero = jnp.zeros((ROWS, width), F32)
        for k1 in range(FFT_K1):
            re, im = spec[k1]
            a_ref[k1, pl.ds(r, ROWS), :] = zero if re is None else re
            a_ref[k1, pl.ds(FFT_N2 + r, ROWS), :] = zero if im is None else im
        return carry
    lax.fori_loop(0, FFT_N2 // ROWS, body, 0)


def _tile_lanes(t, width):
    reps = width // t.shape[-1]
    return t if reps == 1 else jnp.concatenate([t] * reps, axis=-1)


def _mod_kernel(c_ref, w_ref, b_ref, o_ref):
    c = c_ref[...]
    s = c * jax.nn.sigmoid(c)
    o_ref[...] = _dot(s, w_ref[...], HI) + b_ref[...]


def _mod_call(cc, w, b):
    n = w.shape[1]
    bn = 1024
    return pl.pallas_call(
        _mod_kernel,
        grid=(n // bn,),
        in_specs=[pl.BlockSpec(cc.shape, lambda j: (0, 0)),
                  pl.BlockSpec((D_MODEL, bn), lambda j: (0, j)),
                  pl.BlockSpec((1, bn), lambda j: (0, j))],
        out_specs=pl.BlockSpec((cc.shape[0], bn), lambda j: (0, j)),
        out_shape=jax.ShapeDtypeStruct((cc.shape[0], n), F32),
        compiler_params=pltpu.CompilerParams(vmem_limit_bytes=VMEM_LIMIT),
        name="mod",
    )(cc, w, b)


def _rms(x):
    return x * lax.rsqrt(jnp.mean(x * x, axis=-1, keepdims=True) + EPS)


def _short_conv(x, w, b):
    n = x.shape[0]
    pos = lax.broadcasted_iota(jnp.int32, x.shape, 0) & (GRID_W - 1)
    prev = jnp.where(pos == 0, 0.0, pltpu.roll(x, 1, 0))
    nxt = jnp.where(pos == GRID_W - 1, 0.0, pltpu.roll(x, n - 1, 0))
    return b + w[0:1, :] * prev + w[1:2, :] * x + w[2:3, :] * nxt


def _in_proj_kernel(*refs, latent):
    if latent:
        (x_ref, mod_ref, g_ref, w_ref, gw_ref, gb_ref, cw_ref, cb_ref,
         x0_ref, u_ref, q_ref, k_ref, v_ref, og_ref, lgf_ref, lgb_ref) = refs
    else:
        x_ref, mod_ref, g_ref, w_ref, gw_ref, gb_ref, k_ref, v_ref, lgf_ref, lgb_ref = refs
    x = x_ref[0]
    sh = mod_ref[0, :, 0:D_MODEL]
    sc = mod_ref[0, :, D_MODEL:2 * D_MODEL]
    h = ((_rms(x) * g_ref[...]) * (1.0 + sc) + sh).astype(BF16)
    c0 = 3 * D_HYENA
    c1 = c0 + GLA_DK
    c2 = c1 + GLA_DK
    c3 = c2 + GLA_DV
    gr = _dot(h, w_ref[:, GATE_COL:N_IN_PAD]).astype(BF16)
    z = _dot(gr, gw_ref[...]) + gb_ref[...]
    lg = jax.nn.log_sigmoid(z) * (1.0 / GATE_NORMALIZER)
    lgf_ref[0] = lg[:, 0:GLA_DK]
    lgb_ref[0] = lg[:, GLA_DK:2 * GLA_DK]
    if latent:
        dh = D_HYENA
        conv = [_short_conv(_dot(h, w_ref[:, j * dh:(j + 1) * dh]),
                            cw_ref[:, j * dh:(j + 1) * dh], cb_ref[:, j * dh:(j + 1) * dh])
                for j in range(3)]
        x0_ref[0] = conv[0].astype(BF16)
        u_ref[0] = (conv[1] * conv[2]).astype(BF16)
        q = _dot(h, w_ref[:, c0:c1]) * (GLA_HEAD_DK ** -0.5)
        q_ref[0] = q.astype(BF16)
        og_ref[0] = _dot(h, w_ref[:, c3:c3 + GLA_DV]).astype(BF16)
    k_ref[0] = _dot(h, w_ref[:, c1:c2]).astype(BF16)
    v_ref[0] = _dot(h, w_ref[:, c2:c3]).astype(BF16)


def _in_proj_call(x, mod, g, w, gw, gb, conv, tm):
    bsz, t, _ = x.shape
    latent = conv is not None
    def tile(width):
        return pl.BlockSpec((1, tm, width), lambda b, i: (b, i, 0))
    def whole(a):
        return pl.BlockSpec(a.shape, lambda b, i: (0,) * a.ndim)
    if latent:
        widths = (D_HYENA, D_HYENA, GLA_DK, GLA_DK, GLA_DV, GLA_DV, GLA_DK, GLA_DK)
        dtypes = (BF16,) * 6 + (F32, F32)
    else:
        widths = (GLA_DK, GLA_DV, GLA_DK, GLA_DK)
        dtypes = (BF16, BF16, F32, F32)
    args = [x, mod, g, w, gw, gb] + (list(conv) if latent else [])
    return pl.pallas_call(
        functools.partial(_in_proj_kernel, latent=latent),
        grid=(bsz, t // tm),
        in_specs=[tile(D_MODEL),
                  pl.BlockSpec((1, 1, 6 * D_MODEL), lambda b, i: (b, 0, 0))]
                 + [whole(a) for a in args[2:]],
        out_specs=[tile(wd) for wd in widths],
        out_shape=[jax.ShapeDtypeStruct((bsz, t, wd), dt) for wd, dt in zip(widths, dtypes)],
        compiler_params=pltpu.CompilerParams(
            dimension_semantics=("parallel", "parallel"), vmem_limit_bytes=VMEM_LIMIT),
        name="in_proj" if latent else "in_proj_ctx",
    )(*args)


def _ctx_state_kernel(k_ref, v_ref, lgf_ref, lgb_ref, sf_ref, sb_ref):
    t = k_ref.shape[1]
    row = lax.broadcasted_iota(jnp.int32, (t, t), 0)
    col = lax.broadcasted_iota(jnp.int32, (t, t), 1)
    k = k_ref[0].astype(F32)
    v = v_ref[0]
    for lg_ref, tri, o_ref in ((lgf_ref, col > row, sf_ref), (lgb_ref, col < row, sb_ref)):
        rem = _dot(tri.astype(F32), lg_ref[0], HI)
        kend = (k * jnp.exp(rem)).astype(BF16)
        ds = lax.dot_general(kend, v, (((0,), (0,)), ((), ())), preferred_element_type=F32)
        for h in range(GLA_HEADS):
            o_ref[0, h] = ds[h * GLA_HEAD_DK:(h + 1) * GLA_HEAD_DK,
                             h * GLA_HEAD_DV:(h + 1) * GLA_HEAD_DV]


def _ctx_state_call(k, v, lgf, lgb):
    bsz, t, _ = k.shape
    def tile(width):
        return pl.BlockSpec((1, t, width), lambda b: (b, 0, 0))
    st = pl.BlockSpec((1, GLA_HEADS, GLA_HEAD_DK, GLA_HEAD_DV), lambda b: (b, 0, 0, 0))
    return pl.pallas_call(
        _ctx_state_kernel,
        grid=(bsz,),
        in_specs=[tile(GLA_DK), tile(GLA_DV), tile(GLA_DK), tile(GLA_DK)],
        out_specs=[st, st],
        out_shape=[jax.ShapeDtypeStruct((bsz, GLA_HEADS, GLA_HEAD_DK, GLA_HEAD_DV), F32)] * 2,
        compiler_params=pltpu.CompilerParams(vmem_limit_bytes=VMEM_LIMIT),
        name="gla_ctx",
    )(k, v, lgf, lgb)


def _prefix_rows(x, reverse):
    n = x.shape[0]
    pos = lax.broadcasted_iota(jnp.int32, x.shape, 0) & (GLA_CHUNK - 1)
    k = 1
    while k < GLA_CHUNK:
        if reverse:
            x = x + jnp.where(pos < GLA_CHUNK - k, pltpu.roll(x, n - k, 0), 0.0)
        else:
            x = x + jnp.where(pos >= k, pltpu.roll(x, k, 0), 0.0)
        k *= 2
    return x


def _pair_diag(a, b):
    z = jnp.zeros_like(a)
    return jnp.concatenate([jnp.concatenate([a, z], axis=1),
                            jnp.concatenate([z, b], axis=1)], axis=0)


def _gla_stage1(q, k, v, b, reverse):
    c, hd, hv = GLA_CHUNK, GLA_HEAD_DK, GLA_HEAD_DV
    total = b[0:1] if reverse else b[c - 1:c]
    qin = (q * jnp.exp(b)).astype(BF16)
    kin = k * jnp.exp(-b)
    kend = k * jnp.exp(total - b)
    dec = jnp.exp(total)
    hi = dec.astype(BF16).astype(F32)
    mid = (dec - hi).astype(BF16).astype(F32)
    lo = dec - hi - mid
    r8 = lax.broadcasted_iota(jnp.int32, (8, GLA_DK), 0)
    extra = jnp.where(r8 == 0, hi, jnp.where(r8 == 1, mid, jnp.where(r8 == 2, lo, 0.0)))
    a = jnp.concatenate([kend, extra, jnp.zeros((c - 8, GLA_DK), F32)], axis=0)
    at = a.T.astype(BF16)
    r64 = lax.broadcasted_iota(jnp.int32, (c, 2 * hv), 0)
    l64 = lax.broadcasted_iota(jnp.int32, (c, 2 * hv), 1)
    ones_blk = jnp.where((r64 < 3) & (l64 >= hv), 1.0, 0.0).astype(BF16)
    zv = jnp.zeros((c, hv), BF16)
    key = lax.broadcasted_iota(jnp.int32, (c, 2 * hd), 1)
    qrow = lax.broadcasted_iota(jnp.int32, (c, 2 * hd), 0)
    att_mask = ((key & (hd - 1)) >= qrow) if reverse else ((key & (hd - 1)) <= qrow)
    atts = []
    for p in range(GLA_HEADS // 2):
        lanes = slice(p * 2 * hd, (p + 1) * 2 * hd)
        k2 = kin[:, lanes]
        kbd = jnp.concatenate([jnp.where(key < hd, k2, 0.0), jnp.where(key >= hd, k2, 0.0)],
                              axis=0).astype(BF16)
        att = lax.dot_general(qin[:, lanes], kbd, (((1,), (1,)), ((), ())),
                              preferred_element_type=F32)
        atts.append(jnp.where(att_mask, att, 0.0).astype(BF16))
    incs = []
    for h in range(GLA_HEADS):
        vh = v[:, h * hv:(h + 1) * hv]
        incs.append(_dot(at[h * hd:(h + 1) * hd, :],
                         jnp.concatenate([jnp.concatenate([vh, zv], axis=1), ones_blk], axis=0)))
    return qin, atts, incs


def _gla_stage2(qin, atts, incs, v, s):
    hd, hv = GLA_HEAD_DK, GLA_HEAD_DV
    outs = []
    for p in range(GLA_HEADS // 2):
        h0, h1 = 2 * p, 2 * p + 1
        rhs = jnp.concatenate([_pair_diag(v[:, h0 * hv:(h0 + 1) * hv], v[:, h1 * hv:(h1 + 1) * hv]),
                               _pair_diag(s[h0].astype(BF16), s[h1].astype(BF16))], axis=0)
        lhs = jnp.concatenate([atts[p], qin[:, p * 2 * hd:(p + 1) * 2 * hd]], axis=1)
        outs.append(_dot(lhs, rhs))
    s_new = [inc[:, hv:] * sh + inc[:, :hv] for inc, sh in zip(incs, s)]
    return jnp.concatenate(outs, axis=1), s_new


def _gla_kernel(*refs, reverse):
    if reverse:
        q_ref, k_ref, v_ref, lg_ref, s0_ref, of_ref, og_ref, gn_ref, o_ref, s_scr, b_scr = refs
    else:
        q_ref, k_ref, v_ref, lg_ref, s0_ref, o_ref, s_scr, b_scr = refs
    c = GLA_CHUNK

    @pl.when(pl.program_id(1) == 0)
    def _():
        s_scr[...] = s0_ref[0]

    b_scr[...] = _prefix_rows(lg_ref[0], reverse)
    n_chunks = q_ref.shape[1] // c
    order = list(range(n_chunks - 1, -1, -1) if reverse else range(n_chunks))

    def start(ci):
        rows = slice(ci * c, (ci + 1) * c)
        return _gla_stage1(q_ref[0, rows, :].astype(F32), k_ref[0, rows, :].astype(F32),
                           v_ref[0, rows, :], b_scr[rows, :], reverse)

    def finish(ci, staged, s):
        rows = slice(ci * c, (ci + 1) * c)
        o, s = _gla_stage2(*staged, v_ref[0, rows, :], s)
        if reverse:
            o = o + of_ref[0, rows, :]
            og = og_ref[0, rows, :].astype(F32)
            parts = []
            for h in range(GLA_HEADS):
                oh = o[:, h * GLA_HEAD_DV:(h + 1) * GLA_HEAD_DV]
                parts.append(_rms(oh) * gn_ref[...])
            y = jnp.concatenate(parts, axis=1) * (og * jax.nn.sigmoid(og))
            o_ref[0, rows, :] = y.astype(o_ref.dtype)
        else:
            o_ref[0, rows, :] = o
        return s

    s = [s_scr[h] for h in range(GLA_HEADS)]
    staged = start(order[0])
    for prev, ci in zip(order, order[1:]):
        nxt = start(ci)
        s = finish(prev, staged, s)
        staged = nxt
    s = finish(order[-1], staged, s)
    for h in range(GLA_HEADS):
        s_scr[h] = s[h]


def _gla_call(q, k, v, lg, s0, extra, reverse, tt):
    bsz, t, _ = q.shape
    nt = t // tt
    if reverse:
        def tile(width):
            return pl.BlockSpec((1, tt, width), lambda b, j: (b, nt - 1 - j, 0))
    else:
        def tile(width):
            return pl.BlockSpec((1, tt, width), lambda b, j: (b, j, 0))
    in_specs = [tile(GLA_DK), tile(GLA_DK), tile(GLA_DV), tile(GLA_DK),
                pl.BlockSpec((1, GLA_HEADS, GLA_HEAD_DK, GLA_HEAD_DV), lambda b, j: (b, 0, 0, 0))]
    args = [q, k, v, lg, s0]
    if reverse:
        o_f, og, gn = extra
        in_specs += [tile(GLA_DV), tile(GLA_DV), pl.BlockSpec(gn.shape, lambda b, j: (0, 0))]
        args += [o_f, og, gn]
    return pl.pallas_call(
        functools.partial(_gla_kernel, reverse=reverse),
        grid=(bsz, nt),
        in_specs=in_specs,
        out_specs=tile(GLA_DV),
        out_shape=jax.ShapeDtypeStruct((bsz, t, GLA_DV), BF16 if reverse else F32),
        scratch_shapes=[pltpu.VMEM((GLA_HEADS, GLA_HEAD_DK, GLA_HEAD_DV), F32),
                        pltpu.VMEM((tt, GLA_DK), F32)],
        compiler_params=pltpu.CompilerParams(
            dimension_semantics=("parallel", "arbitrary"), vmem_limit_bytes=VMEM_LIMIT),
        name="gla_bwd" if reverse else "gla_fwd",
    )(*args)


def _split_bf16(x):
    hi = x.astype(BF16)
    return hi, (x - hi.astype(F32)).astype(BF16)


def _dot_split(a_hi, a_lo, b_hi, b_lo):
    return _dot(a_hi, b_hi) + _dot(a_hi, b_lo) + _dot(a_lo, b_hi)


def _filter_mlp_kernel(z_ref, w1_ref, b1_ref, w2_ref, b2_ref, w3_ref, b3_ref, fr_ref, o_ref):
    fr = fr_ref[...]
    h = jnp.sin(fr * (_dot(z_ref[...], w1_ref[...], HI) + b1_ref[...]))
    h = jnp.sin(fr * (_dot(h, w2_ref[...], HI) + b2_ref[...]))
    o_ref[...] = jnp.sin(fr * (_dot(h, w3_ref[...], HI) + b3_ref[...]))


def _filter_mlp_call(zp, w1, b1, w2, b2, w3, b3, fr):
    return pl.pallas_call(
        _filter_mlp_kernel,
        out_shape=jax.ShapeDtypeStruct((zp.shape[0], 2 * FILTER_HIDDEN), F32),
        compiler_params=pltpu.CompilerParams(vmem_limit_bytes=VMEM_LIMIT),
        name="filter_mlp",
    )(zp, w1, b1, w2, b2, w3, b3, fr)


def _filter_kernel(hdn_ref, wfh_ref, wfl_ref, wbh_ref, wbl_ref, t_ref, dl_ref, f2h_ref, f2l_ref,
                   twr_ref, twi_ref, khat_ref, kern_ref, a_ref):
    dl = dl_ref[...]
    blk = 512

    def taps_body(i, ss):
        r0 = pl.multiple_of(i * blk, blk)
        rows = pl.ds(r0, blk)
        hf = _dot_split(*_split_bf16(hdn_ref[0, rows, :]), wfh_ref[...], wfl_ref[...])
        hb = _dot_split(*_split_bf16(hdn_ref[1, rows, :]), wbh_ref[...], wbl_ref[...])
        hf = hf * jnp.exp(-_tile_lanes(t_ref[0, rows, :], CG) * dl)
        hb = hb * jnp.exp(-_tile_lanes(t_ref[1, rows, :], CG) * dl)
        lag = r0 + lax.broadcasted_iota(jnp.int32, hb.shape, 0)
        hb = jnp.where(lag == 0, 0.0, hb)
        kern_ref[rows, :] = hf
        kern_ref[pl.ds(SEQ + r0, blk), :] = hb
        return (ss + jnp.sum(hf * hf, axis=0, keepdims=True)
                + jnp.sum(hb * hb, axis=0, keepdims=True))
    ss = lax.fori_loop(0, SEQ // blk, taps_body, jnp.zeros((1, CG), F32))
    scale = lax.rsqrt(ss + EPS) * (1.0 / FFT_N)

    def load_slab(s, r):
        return kern_ref[pl.ds(s * FFT_N2 + r, ROWS), :]
    _slow_forward(load_slab, FFT_N1, a_ref, CG)

    def body(k1, carry):
        ar = a_ref[k1, 0:FFT_N2, :]
        ai = a_ref[k1, FFT_N2:2 * FFT_N2, :]
        twr = _tile_lanes(twr_ref[k1], CG)
        twi = _tile_lanes(twi_ref[k1], CG)
        a = jnp.concatenate([ar * twr - ai * twi, ar * twi + ai * twr], axis=0)
        x = _dot_split(f2h_ref[...], f2l_ref[...], *_split_bf16(a))
        herm = jnp.where((k1 == 0) | (k1 == FFT_K1 - 1), 1.0, 2.0)
        khat_ref[k1] = x * (scale * herm)
        return carry
    lax.fori_loop(0, FFT_K1, body, 0)


def _filter_call(hdn, wout_hi, wout_lo, t2, deltas, f2_hi, f2_lo, twr, twi):
    ng = D_HYENA // CG
    def whole(a):
        return pl.BlockSpec(a.shape, lambda g: (0,) * a.ndim)
    fwd_cols = pl.BlockSpec((FILTER_HIDDEN, CG), lambda g: (0, g))
    bwd_cols = pl.BlockSpec((FILTER_HIDDEN, CG), lambda g: (0, ng + g))
    return pl.pallas_call(
        _filter_kernel,
        grid=(ng,),
        in_specs=[whole(hdn), fwd_cols, fwd_cols, bwd_cols, bwd_cols,
                  whole(t2),
                  pl.BlockSpec((1, CG), lambda g: (0, g)),
                  whole(f2_hi), whole(f2_lo), whole(twr), whole(twi)],
        out_specs=pl.BlockSpec((FFT_K1, 2 * FFT_N2, CG), lambda g: (0, 0, g)),
        out_shape=jax.ShapeDtypeStruct((FFT_K1, 2 * FFT_N2, D_HYENA), F32),
        scratch_shapes=[pltpu.VMEM((FFT_N, CG), F32),
                        pltpu.VMEM((FFT_K1, 2 * FFT_N2, CG), F32)],
        compiler_params=pltpu.CompilerParams(vmem_limit_bytes=VMEM_LIMIT),
        name="filter_spec",
    )(hdn, wout_hi, wout_lo, wout_hi, wout_lo, t2, deltas, f2_hi, f2_lo, twr, twi)


def _freq_forward(k1, a_ref, ff_ref, twr_ref, twi_ref):
    ar = a_ref[k1, 0:FFT_N2, :]
    ai = a_ref[k1, FFT_N2:2 * FFT_N2, :]
    twr = _tile_lanes(twr_ref[k1], CG)
    twi = _tile_lanes(twi_ref[k1], CG)
    a = jnp.concatenate([ar * twr - ai * twi, ar * twi + ai * twr], axis=0).astype(BF16)
    return _dot(ff_ref[...], a)


def _freq_inverse(k1, x, a_ref, khat_ref, fi_ref, twr_ref, twi_ref):
    xr, xi = x[0:FFT_N2], x[FFT_N2:]
    kr = khat_ref[k1, 0:FFT_N2, :]
    ki = khat_ref[k1, FFT_N2:2 * FFT_N2, :]
    y = jnp.concatenate([xr * kr - xi * ki, xr * ki + xi * kr], axis=0).astype(BF16)
    bq = _dot(fi_ref[...], y)
    br, bi = bq[0:FFT_N2], bq[FFT_N2:]
    twr = _tile_lanes(twr_ref[k1], CG)
    twi = _tile_lanes(twi_ref[k1], CG)
    a_ref[k1, 0:FFT_N2, :] = br * twr + bi * twi
    a_ref[k1, FFT_N2:2 * FFT_N2, :] = bi * twr - br * twi


def _hyena_kernel(x0_ref, u_ref, bias_ref, khat_ref, ff_ref, fi_ref, twr_ref, twi_ref,
                  o_ref, a_ref):
    assert ROWS % 16 == 0

    def load_slab(s, r):
        return u_ref[0, pl.ds(s * FFT_N2 + r, ROWS), :].astype(F32)
    _slow_forward(load_slab, FFT_N1 // 2, a_ref, CG)

    x_next = _freq_forward(0, a_ref, ff_ref, twr_ref, twi_ref)
    for k1 in range(FFT_K1):
        x = x_next
        if k1 + 1 < FFT_K1:
            x_next = _freq_forward(k1 + 1, a_ref, ff_ref, twr_ref, twi_ref)
        _freq_inverse(k1, x, a_ref, khat_ref, fi_ref, twr_ref, twi_ref)

    bias = bias_ref[...]

    net_in = [_Lin() for _ in range(FFT_N1 // 2)]
    net_out = [part for pair in _rfft_sym(net_in + [None] * (FFT_N1 // 2)) for part in pair]

    def inv_body(i, carry):
        r = pl.multiple_of(i * ROWS, ROWS)
        cots = [a_ref[k1, pl.ds(half * FFT_N2 + r, ROWS), :] if node is not None else None
                for k1 in range(FFT_K1) for half, node in enumerate(net_out[2 * k1:2 * k1 + 2])]
        ys = _transpose_apply(net_in, net_out, cots)
        for s in range(FFT_N1 // 2):
            rows = pl.ds(s * FFT_N2 + r, ROWS)
            conv = ys[s] + bias * u_ref[0, rows, :].astype(F32)
            o_ref[0, rows, :] = (x0_ref[0, rows, :].astype(F32) * conv).astype(o_ref.dtype)
        return carry
    lax.fori_loop(0, FFT_N2 // ROWS, inv_body, 0)


def _hyena_call(x0c, u, bias, khat, ff, fi, twr, twi):
    bsz = u.shape[0]
    ng = D_HYENA // CG
    stream = pl.BlockSpec((1, SEQ, CG), lambda g, b: (b, 0, g))
    def whole(a):
        return pl.BlockSpec(a.shape, lambda g, b: (0,) * a.ndim)
    return pl.pallas_call(
        _hyena_kernel,
        grid=(ng, bsz),
        in_specs=[stream, stream,
                  pl.BlockSpec((1, CG), lambda g, b: (0, g)),
                  pl.BlockSpec((FFT_K1, 2 * FFT_N2, CG), lambda g, b: (0, 0, g)),
                  whole(ff), whole(fi), whole(twr), whole(twi)],
        out_specs=stream,
        out_shape=jax.ShapeDtypeStruct((bsz, SEQ, D_HYENA), BF16),
        scratch_shapes=[pltpu.VMEM((FFT_K1, 2 * FFT_N2, CG), F32)],
        compiler_params=pltpu.CompilerParams(
            dimension_semantics=("parallel", "parallel"), vmem_limit_bytes=VMEM_LIMIT),
        name="hyena",
    )(x0c, u, bias, khat, ff, fi, twr, twi)


def _out_kernel(x_ref, yh_ref, yg_ref, mod_ref, wo_ref, nm_ref, w1_ref, w2_ref, nf_ref, o_ref):
    d = D_MODEL
    g1 = mod_ref[0, :, 2 * d:3 * d]
    sh2 = mod_ref[0, :, 3 * d:4 * d]
    sc2 = mod_ref[0, :, 4 * d:5 * d]
    g2 = mod_ref[0, :, 5 * d:6 * d]
    mix = _dot(yh_ref[0], wo_ref[0:D_HYENA, :]) + _dot(yg_ref[0], wo_ref[D_HYENA:d, :])
    x1 = x_ref[0] + g1 * mix
    h2 = ((_rms(x1) * nm_ref[...]) * (1.0 + sc2) + sh2).astype(BF16)
    acc = jnp.zeros(x1.shape, F32)
    for j in range(D_FF // d):
        hid = _dot(h2, w1_ref[:, j * d:(j + 1) * d])
        hid = jnp.square(jnp.maximum(hid, 0.0)).astype(BF16)
        acc = acc + _dot(hid, w2_ref[j * d:(j + 1) * d, :])
    x2 = x1 + g2 * acc
    o_ref[0] = _rms(x2) * nf_ref[...]


def _out_call(x, yh, yg, mod, wo, nm, w1, w2, nf, tm):
    bsz, t, _ = x.shape
    def tile(width):
        return pl.BlockSpec((1, tm, width), lambda b, i: (b, i, 0))
    def whole(a):
        return pl.BlockSpec(a.shape, lambda b, i: (0,) * a.ndim)
    return pl.pallas_call(
        _out_kernel,
        grid=(bsz, t // tm),
        in_specs=[tile(D_MODEL), tile(D_HYENA), tile(GLA_DV),
                  pl.BlockSpec((1, 1, 6 * D_MODEL), lambda b, i: (b, 0, 0)),
                  whole(wo), whole(nm), whole(w1), whole(w2), whole(nf)],
        out_specs=tile(D_MODEL),
        out_shape=jax.ShapeDtypeStruct((bsz, t, D_MODEL), F32),
        compiler_params=pltpu.CompilerParams(
            dimension_semantics=("parallel", "parallel"), vmem_limit_bytes=VMEM_LIMIT),
        name="out_mlp",
    )(x, yh, yg, mod, wo, nm, w1, w2, nf)


def _lag_reverse(a):
    return jnp.concatenate([a[0:1], a[:0:-1]], axis=0)


def _pair(w):
    z = jnp.zeros_like(w)
    return jnp.block([[w, z], [z, w]])


def _filter_features():
    length = SEQ
    t = jnp.linspace(0.0, 1.0, length, dtype=F32)[:, None]
    w = 2.0 * math.pi * jnp.arange(length, dtype=F32) / length
    f = jnp.linspace(1e-4, FILTER_BANDS - 1, FILTER_BANDS, dtype=F32)
    ang = w[:, None] * f[None, :]
    z = jnp.concatenate([t, jnp.cos(ang), -jnp.sin(ang)], axis=-1)
    z = jnp.pad(z, ((0, 0), (0, 128 - FILTER_EMB)))
    zp = jnp.concatenate([z[:length // 2], z[length // 2:]], axis=1)
    t2 = jnp.broadcast_to(jnp.stack([t, _lag_reverse(t)]), (2, length, 128))
    min_decay = math.log(DECAY_TARGET) / SLOW_DECAY_PCT
    max_decay = math.log(DECAY_TARGET) / FAST_DECAY_PCT
    deltas = jnp.abs(jnp.linspace(min_decay, max_decay, D_HYENA, dtype=F32))[None, :]
    return zp, t2, deltas


def _dft_tables():
    n2 = jnp.arange(FFT_N2, dtype=jnp.int32)
    k1 = jnp.arange(FFT_K1, dtype=jnp.int32)
    ang_tw = (2.0 * math.pi / FFT_N) * (k1[:, None] * n2[None, :]).astype(F32)
    twr = jnp.broadcast_to(jnp.cos(ang_tw)[:, :, None], (FFT_K1, FFT_N2, 128))
    twi = jnp.broadcast_to(-jnp.sin(ang_tw)[:, :, None], (FFT_K1, FFT_N2, 128))
    ang = (2.0 * math.pi / FFT_N2) * ((n2[:, None] * n2[None, :]) % FFT_N2).astype(F32)
    cr, ci = jnp.cos(ang), -jnp.sin(ang)
    fwd = jnp.block([[cr, -ci], [ci, cr]])
    inv = jnp.block([[cr, ci], [-ci, cr]])
    return fwd, inv, twr, twi


def kernel(x, c, ctx, c_ctx, w_ada, b_ada, norm_mix, norm_mlp, w_in, conv_w, conv_b, filt_w1, filt_b1, filt_w2, filt_b2, filt_w3, filt_b3, filt_freq, filt_wout, hyena_bias, gk_w_fwd, gk_b_fwd, gk_w_bwd, gk_b_bwd, gla_norm, w_out, w_mlp1, w_mlp2, norm_final):
    bsz = x.shape[0]
    l = 0
    cc = jnp.concatenate([c, c_ctx[None, :], jnp.zeros((16 - bsz - 1, D_MODEL), F32)], axis=0)
    mod_all = _mod_call(cc, w_ada[l], b_ada[l][None, :])
    mod = mod_all[:bsz][:, None, :]
    mod_c = jnp.broadcast_to(mod_all[bsz][None, None, :], (bsz, 1, 6 * D_MODEL))

    w_in_p = jnp.pad(w_in[l], ((0, 0), (0, N_IN_PAD - N_IN))).astype(BF16)
    gw = jnp.zeros((N_IN_PAD - GATE_COL, 2 * GLA_DK), F32)
    gw = gw.at[0:GATE_RANK, 0:GLA_DK].set(gk_w_fwd[l])
    gw = gw.at[GATE_RANK:2 * GATE_RANK, GLA_DK:].set(gk_w_bwd[l])
    gb = jnp.concatenate([gk_b_fwd[l], gk_b_bwd[l]])[None, :]
    gw = gw.astype(BF16)
    g_mix = norm_mix[l][None, :]

    conv = (conv_w[l], conv_b[l][None, :])
    x0c, u, q, k, v, og, lgf, lgb = _in_proj_call(x, mod, g_mix, w_in_p, gw, gb, conv, 512)
    k_c, v_c, lgf_c, lgb_c = _in_proj_call(ctx, mod_c, g_mix, w_in_p, gw, gb, None, ctx.shape[1])

    s_f, s_b = _ctx_state_call(k_c, v_c, lgf_c, lgb_c)
    o_f = _gla_call(q, k, v, lgf, s_f, None, False, 2048)
    y_gla = _gla_call(q, k, v, lgb, s_b, (o_f, og, gla_norm[l][None, :]), True, 2048)

    zp, t2, deltas = _filter_features()
    f2_fwd, f2_inv, twr, twi = _dft_tables()
    def two(b):
        return jnp.concatenate([b, b])[None, :]
    w1p = jnp.pad(filt_w1[l], ((0, 128 - FILTER_EMB), (0, 0)))
    hp = _filter_mlp_call(zp, _pair(w1p), two(filt_b1[l]), _pair(filt_w2[l]), two(filt_b2[l]),
                          _pair(filt_w3[l]), two(filt_b3[l]), two(filt_freq[l]))
    hdn = jnp.concatenate([hp[:, :FILTER_HIDDEN], hp[:, FILTER_HIDDEN:]], axis=0)
    hdn2 = jnp.stack([hdn, _lag_reverse(hdn)])
    khat = _filter_call(hdn2, *_split_bf16(filt_wout[l]), t2, deltas, *_split_bf16(f2_fwd), twr, twi)
    y_hy = _hyena_call(x0c, u, hyena_bias[l][None, :], khat,
                       f2_fwd.astype(BF16), f2_inv.astype(BF16), twr, twi)

    return _out_call(x, y_hy, y_gla, mod, w_out[l].astype(BF16), norm_mlp[l][None, :],
                     w_mlp1[l].astype(BF16), w_mlp2[l].astype(BF16), norm_final[None, :], 512)
```
